```python
import math
import jax, jax.numpy as jnp
from jax import lax
import numpy as np

D_MODEL = 2048
BATCH = 2
SEQ = 4096
DEPTH = 1
DEC_BATCH = 128
DEC_SEQ = 8
PAST_LEN = 16384
PAGE_SIZE = 128

MLA_HEADS = 16
QK_NOPE = 128
QK_ROPE = 64
V_HEAD = 128
Q_LORA = 512
KV_LORA = 512
ROPE_THETA = 10000.0
SM_SCALE = (QK_NOPE + QK_ROPE) ** -0.5
Q_BLOCK = 128
RW_HEAD = 64
RW_HEADS = D_MODEL // RW_HEAD
RW_WIDTH = RW_HEADS * RW_HEAD
DECAY_LORA = 96
AAA_LORA = 96
GATE_LORA = 256
GN_EPS = 64e-5
RW_COLS = 3 * RW_WIDTH + DECAY_LORA + AAA_LORA + GATE_LORA
RW_SPLITS = (RW_WIDTH, RW_WIDTH + DECAY_LORA, 2 * RW_WIDTH + DECAY_LORA,
             3 * RW_WIDTH + DECAY_LORA, 3 * RW_WIDTH + DECAY_LORA + AAA_LORA)
MLA_COLS = Q_LORA + KV_LORA + QK_ROPE
OFF_GATE = MLA_COLS + RW_COLS
IN_COLS = OFF_GATE + 2 * D_MODEL
D_FF = ((8 * D_MODEL // 3 + 255) // 256) * 256
PLE_DIM = 256
LN_EPS = 1e-5
RMS_EPS = 1e-6
ALPHA = (2.0 * DEPTH) ** 0.25
BETA = (8.0 * DEPTH) ** -0.25

kernel_name = 'mla_rwkv7_gated_parallel_deepnorm'


def layer_norm(x, g, b):
    xf = x.astype(jnp.float32)
    mu = jnp.mean(xf, axis=-1, keepdims=True)
    var = jnp.mean(jnp.square(xf - mu), axis=-1, keepdims=True)
    return ((xf - mu) * lax.rsqrt(var + LN_EPS) * g + b).astype(x.dtype)


def rms_norm(x, g):
    xf = x.astype(jnp.float32)
    return (xf * lax.rsqrt(jnp.mean(jnp.square(xf), axis=-1, keepdims=True) + RMS_EPS) * g).astype(x.dtype)


def rope(x, pos):
    half = QK_ROPE // 2
    inv = ROPE_THETA ** (-jnp.arange(half, dtype=jnp.float32) / half)
    ang = pos[:, None] * inv[None, :]
    shape = (ang.shape[0],) + (1,) * (x.ndim - 3) + (half,)
    cos = jnp.cos(ang).reshape(shape)
    sin = jnp.sin(ang).reshape(shape)
    xf = x.astype(jnp.float32)
    x1, x2 = xf[..., :half], xf[..., half:]
    return jnp.concatenate([x1 * cos - x2 * sin, x1 * sin + x2 * cos], axis=-1).astype(x.dtype)


def mla_prompt_attention(q_nope, q_pe, ckv, kpe, w_uk, w_uv):
    B, T = q_nope.shape[:2]
    nb = T // Q_BLOCK
    k_nope = jnp.einsum('btc,chd->bthd', ckv, w_uk)
    v = jnp.einsum('btc,chd->bthd', ckv, w_uv)
    qn = jnp.moveaxis(q_nope.reshape(B, nb, Q_BLOCK, MLA_HEADS, QK_NOPE), 1, 0)
    qp = jnp.moveaxis(q_pe.reshape(B, nb, Q_BLOCK, MLA_HEADS, QK_ROPE), 1, 0)
    kpos = jnp.arange(T)

    def block(args):
        qn_b, qp_b, i = args
        s = jnp.einsum('bqhd,bkhd->bhqk', qn_b, k_nope) + jnp.einsum('bqhr,bkr->bhqk', qp_b, kpe)
        qpos = i * Q_BLOCK + jnp.arange(Q_BLOCK)
        mask = kpos[None, :] <= qpos[:, None]
        s = jnp.where(mask[None, None], s.astype(jnp.float32) * SM_SCALE, -jnp.inf)
        p = jax.nn.softmax(s, axis=-1).astype(v.dtype)
        return jnp.einsum('bhqk,bkhd->bqhd', p, v)

    o = lax.map(block, (qn, qp, jnp.arange(nb)))
    return jnp.moveaxis(o, 0, 1).reshape(B, T, MLA_HEADS * V_HEAD)


def mla_paged_attention(q_nope, q_pe, ckv_new, kpe_new, cache_ckv, cache_kpe, page_table, layer, w_uk, w_uv):
    B, T = q_nope.shape[:2]
    q_lat = jnp.einsum('bthd,chd->bthc', q_nope, w_uk)
    kidx = jnp.arange(PAST_LEN + T)
    mask = kidx[None, :] <= PAST_LEN + jnp.arange(T)[:, None]

    def one_seq(args):
        pt, ql, qp, cn, kn = args
        keys_c = jnp.concatenate([cache_ckv[layer, pt].reshape(PAST_LEN, KV_LORA), cn], axis=0)
        keys_r = jnp.concatenate([cache_kpe[layer, pt].reshape(PAST_LEN, QK_ROPE), kn], axis=0)
        s = jnp.einsum('thc,kc->htk', ql, keys_c) + jnp.einsum('thr,kr->htk', qp, keys_r)
        s = jnp.where(mask[None], s.astype(jnp.float32) * SM_SCALE, -jnp.inf)
        p = jax.nn.softmax(s, axis=-1).astype(keys_c.dtype)
        return jnp.einsum('htk,kc->thc', p, keys_c)

    o_lat = lax.map(one_seq, (page_table, q_lat, q_pe, ckv_new, kpe_new))
    o = jnp.einsum('bthc,chd->bthd', o_lat, w_uv)
    return o.reshape(B, T, MLA_HEADS * V_HEAD)


def rwkv7_time_mix(zs, wkv_prev, w0, w2, a0, a2, g2, k_k, k_a, r_k, gn_g, gn_b):
    B, T, _ = zs.shape
    f32 = jnp.float32
    r, w_lo, k, v, a_lo, g_lo = jnp.split(zs, list(RW_SPLITS), axis=-1)
    w_log = -jax.nn.softplus(-(w0 + jnp.tanh(w_lo) @ w2)) - 0.5
    a = jax.nn.sigmoid(a0 + a_lo @ a2)
    g = jax.nn.sigmoid(g_lo) @ g2
    hd = lambda t: t.reshape(B, T, RW_HEADS, RW_HEAD)
    kk = hd(k * k_k).astype(f32)
    kk = kk / jnp.maximum(jnp.linalg.norm(kk, axis=-1, keepdims=True), 1e-12)
    k = hd(k * (1.0 + (a - 1.0) * k_a)).astype(f32)
    r, v, a = hd(r).astype(f32), hd(v).astype(f32), hd(a).astype(f32)
    decay = jnp.exp(-jnp.exp(hd(w_log).astype(f32)))
    tm = lambda t: jnp.moveaxis(t, 1, 0)

    def step(S, inp):
        r_t, w_t, k_t, v_t, kk_t, a_t = inp
        sa = jnp.einsum('bhij,bhj->bhi', S, -kk_t)
        S = (S * w_t[:, :, None, :] + sa[..., None] * (kk_t * a_t)[:, :, None, :]
             + v_t[..., None] * k_t[:, :, None, :])
        y = jnp.einsum('bhij,bhj->bhi', S, r_t)
        return S, y

    S, y = lax.scan(step, wkv_prev.astype(f32), (tm(r), tm(decay), tm(k), tm(v), tm(kk), tm(a)))
    y = jnp.moveaxis(y, 0, 1)
    mu = jnp.mean(y, axis=-1, keepdims=True)
    var = jnp.mean(jnp.square(y - mu), axis=-1, keepdims=True)
    y = (y - mu) * lax.rsqrt(var + GN_EPS) * gn_g.reshape(RW_HEADS, RW_HEAD) + gn_b.reshape(RW_HEADS, RW_HEAD)
    y = y + jnp.sum(r * k * r_k, axis=-1, keepdims=True) * v
    return y.reshape(B, T, RW_WIDTH).astype(zs.dtype) * g, S


def mixing_sublayer(x, shift_prev, wkv_prev, pos, attend, lw):
    z = x @ lw['w_in']
    cq = rms_norm(z[..., :Q_LORA], lw['g_q'])
    ckv = rms_norm(z[..., Q_LORA:Q_LORA + KV_LORA], lw['g_kv'])
    kpe = rope(z[..., Q_LORA + KV_LORA:MLA_COLS], pos)
    q = jnp.einsum('btc,chd->bthd', cq, lw['w_uq'])
    q_nope = q[..., :QK_NOPE]
    q_pe = rope(q[..., QK_NOPE:], pos)
    o_mla = attend(q_nope, q_pe, ckv, kpe)
    z_rw = z[..., MLA_COLS:OFF_GATE]
    z_prev = jnp.concatenate([shift_prev[:, None].astype(z.dtype), z_rw[:, :-1]], axis=1)
    zs = z_rw + (z_prev - z_rw) * lw['mu_shift']
    o_rw, wkv_new = rwkv7_time_mix(zs, wkv_prev, lw['rw_w0'], lw['rw_w2'], lw['rw_a0'], lw['rw_a2'],
                                   lw['rw_g2'], lw['rw_kk'], lw['rw_ka'], lw['rw_rk'], lw['gn_g'], lw['gn_b'])
    gate_mla = jax.nn.sigmoid(z[..., OFF_GATE:OFF_GATE + D_MODEL])
    gate_rw = jax.nn.sigmoid(z[..., OFF_GATE + D_MODEL:])
    merged = gate_mla * o_mla + gate_rw * o_rw
    return merged @ lw['w_o'], ckv, kpe, z_rw[:, -1], wkv_new


def trunk_layer(x, pe, shift_prev, wkv_prev, pos, attend, lw):
    mix, ckv, kpe, shift_new, wkv_new = mixing_sublayer(x, shift_prev, wkv_prev, pos, attend, lw)
    x = layer_norm(ALPHA * x + mix, lw['ln1_g'], lw['ln1_b'])
    ffn = (jax.nn.silu(x @ lw['w_ffn_gate']) * (x @ lw['w_ffn_up'])) @ lw['w_ffn_down']
    x = layer_norm(ALPHA * x + ffn, lw['ln2_g'], lw['ln2_b'])
    x = x + jax.nn.sigmoid(x @ lw['w_ple_gate']) * (pe @ lw['w_ple'])
    return x, ckv, kpe, shift_new, wkv_new


def setup_inputs(seed: int = 0) -> dict:
    key = jax.random.key(seed)
    ks = jax.random.split(key, 40)
    f32 = jnp.float32

    def nrm(i, shape, scale):
        return jax.random.normal(ks[i], shape, f32) * scale

    def gain(i, n):
        return 1.0 + 0.02 * jax.random.normal(ks[i], (DEPTH, n), f32)

    n_pages = PAST_LEN // PAGE_SIZE
    n_used = DEC_BATCH * n_pages
    n_phys = n_used + max(1, n_used // 4)
    page_table = jax.random.permutation(ks[0], n_phys)[:n_used].reshape(DEC_BATCH, n_pages).astype(jnp.int32)
    return {
        'x_prompt': nrm(1, (BATCH, SEQ, D_MODEL), 1.0),
        'x_sample': nrm(2, (DEC_BATCH, DEC_SEQ, D_MODEL), 1.0),
        'p_prompt': nrm(3, (DEPTH, BATCH, SEQ, PLE_DIM), 1.0),
        'p_sample': nrm(4, (DEPTH, DEC_BATCH, DEC_SEQ, PLE_DIM), 1.0),
        'cache_ckv': nrm(5, (DEPTH, n_phys, PAGE_SIZE, KV_LORA), 1.0),
        'cache_kpe': nrm(6, (DEPTH, n_phys, PAGE_SIZE, QK_ROPE), 1.0),
        'state_wkv': nrm(7, (DEPTH, DEC_BATCH, RW_HEADS, RW_HEAD, RW_HEAD), 0.5),
        'state_shift': nrm(8, (DEPTH, DEC_BATCH, RW_COLS), 1.0),
        'page_table': page_table,
        'w_in': nrm(9, (DEPTH, D_MODEL, IN_COLS), D_MODEL ** -0.5),
        'mu_shift': jax.random.uniform(ks[10], (DEPTH, RW_COLS), f32),
        'g_q': gain(11, Q_LORA),
        'g_kv': gain(12, KV_LORA),
        'w_uq': nrm(13, (DEPTH, Q_LORA, MLA_HEADS, QK_NOPE + QK_ROPE), Q_LORA ** -0.5),
        'w_uk': nrm(14, (DEPTH, KV_LORA, MLA_HEADS, QK_NOPE), KV_LORA ** -0.5),
        'w_uv': nrm(15, (DEPTH, KV_LORA, MLA_HEADS, V_HEAD), KV_LORA ** -0.5),
        'rw_w0': jax.random.uniform(ks[16], (DEPTH, RW_WIDTH), f32, minval=-6.0, maxval=0.0),
        'rw_w2': nrm(17, (DEPTH, DECAY_LORA, RW_WIDTH), 0.5 * DECAY_LORA ** -0.5),
        'rw_a0': nrm(18, (DEPTH, RW_WIDTH), 0.1),
        'rw_a2': nrm(19, (DEPTH, AAA_LORA, RW_WIDTH), 0.5 * AAA_LORA ** -0.5),
        'rw_g2': nrm(20, (DEPTH, GATE_LORA, RW_WIDTH), GATE_LORA ** -0.5),
        'rw_kk': 0.85 + nrm(21, (DEPTH, RW_WIDTH), 0.02),
        'rw_ka': 1.0 + nrm(22, (DEPTH, RW_WIDTH), 0.02),
        'rw_rk': nrm(23, (DEPTH, RW_HEADS, RW_HEAD), 0.1),
        'gn_g': gain(24, RW_WIDTH),
        'gn_b': nrm(25, (DEPTH, RW_WIDTH), 0.02),
        'w_o': nrm(26, (DEPTH, D_MODEL, D_MODEL), BETA * D_MODEL ** -0.5),
        'ln1_g': gain(27, D_MODEL),
        'ln1_b': nrm(28, (DEPTH, D_MODEL), 0.02),
        'w_ffn_gate': nrm(29, (DEPTH, D_MODEL, D_FF), D_MODEL ** -0.5),
        'w_ffn_up': nrm(30, (DEPTH, D_MODEL, D_FF), D_MODEL ** -0.5),
        'w_ffn_down': nrm(31, (DEPTH, D_FF, D_MODEL), BETA * D_FF ** -0.5),
        'ln2_g': gain(32, D_MODEL),
        'ln2_b': nrm(33, (DEPTH, D_MODEL), 0.02),
        'w_ple': nrm(34, (DEPTH, PLE_DIM, D_MODEL), PLE_DIM ** -0.5),
        'w_ple_gate': nrm(35, (DEPTH, D_MODEL, D_MODEL), D_MODEL ** -0.5),
    }


def reference(x_prompt, x_sample, p_prompt, p_sample, cache_ckv, cache_kpe, state_wkv, state_shift,
              page_table, w_in, mu_shift, g_q, g_kv, w_uq, w_uk, w_uv, rw_w0, rw_w2, rw_a0, rw_a2,
              rw_g2, rw_kk, rw_ka, rw_rk, gn_g, gn_b, w_o, ln1_g, ln1_b, w_ffn_gate, w_ffn_up,
              w_ffn_down, ln2_g, ln2_b, w_ple, w_ple_gate):
    bp, tp = x_prompt.shape[:2]
    ts = x_sample.shape[1]
    pos_p = jnp.arange(tp, dtype=jnp.float32)
    pos_s = PAST_LEN + jnp.arange(ts, dtype=jnp.float32)
    hp, hs = x_prompt, x_sample
    ckv_p, kpe_p, wkv_p, sh_p = [], [], [], []
    ckv_s, kpe_s, wkv_s, sh_s = [], [], [], []
    for l in range(DEPTH):
        lw = dict(w_in=w_in[l], mu_shift=mu_shift[l], g_q=g_q[l], g_kv=g_kv[l], w_uq=w_uq[l],
                  rw_w0=rw_w0[l], rw_w2=rw_w2[l], rw_a0=rw_a0[l], rw_a2=rw_a2[l], rw_g2=rw_g2[l],
                  rw_kk=rw_kk[l], rw_ka=rw_ka[l], rw_rk=rw_rk[l], gn_g=gn_g[l], gn_b=gn_b[l],
                  w_o=w_o[l], ln1_g=ln1_g[l], ln1_b=ln1_b[l], w_ffn_gate=w_ffn_gate[l],
                  w_ffn_up=w_ffn_up[l], w_ffn_down=w_ffn_down[l], ln2_g=ln2_g[l], ln2_b=ln2_b[l],
                  w_ple=w_ple[l], w_ple_gate=w_ple_gate[l])
        uk, uv = w_uk[l], w_uv[l]
        attend_p = lambda qn, qp, c, r: mla_prompt_attention(qn, qp, c, r, uk, uv)
        attend_s = lambda qn, qp, c, r: mla_paged_attention(qn, qp, c, r, cache_ckv, cache_kpe,
                                                           page_table, l, uk, uv)
        shift0 = jnp.zeros((bp, RW_COLS), hp.dtype)
        wkv0 = jnp.zeros((bp, RW_HEADS, RW_HEAD, RW_HEAD), jnp.float32)
        hp, c1, k1, s1, w1 = trunk_layer(hp, p_prompt[l], shift0, wkv0, pos_p, attend_p, lw)
        hs, c2, k2, s2, w2 = trunk_layer(hs, p_sample[l], state_shift[l], state_wkv[l], pos_s, attend_s, lw)
        ckv_p.append(c1); kpe_p.append(k1); sh_p.append(s1); wkv_p.append(w1)
        ckv_s.append(c2); kpe_s.append(k2); sh_s.append(s2); wkv_s.append(w2)
    return (hp, hs,
            jnp.stack(ckv_p), jnp.stack(kpe_p), jnp.stack(wkv_p), jnp.stack(sh_p),
            jnp.stack(ckv_s), jnp.stack(kpe_s), jnp.stack(wkv_s), jnp.stack(sh_s))
```

```python
import functools

import jax
import jax.numpy as jnp
from jax import lax
from jax.experimental import pallas as pl
from jax.experimental.pallas import tpu as pltpu

F32 = jnp.float32
BF16 = jnp.bfloat16

MLA_HEADS = 16
QK_NOPE = 128
QK_ROPE = 64
V_HEAD = 128
RW_HEAD = 64
DECAY_LORA = 96
AAA_LORA = 96
GATE_LORA = 256
ROPE_THETA = 10000.0
SM_SCALE = (QK_NOPE + QK_ROPE) ** -0.5
GN_EPS = 64e-5
LN_EPS = 1e-5
RMS_EPS = 1e-6
PROMPT_CHUNK = 64

LANES = 128
VMEM_LIMIT_BYTES = 56 * 1024 * 1024

NEG_INF = float("-inf")


def _cparams(sem):
    return pltpu.CompilerParams(dimension_semantics=sem, vmem_limit_bytes=VMEM_LIMIT_BYTES)


def _dot(a, b):
    return jnp.dot(a.astype(BF16), b.astype(BF16), preferred_element_type=F32)


def _dot_nt(a, b):
    return lax.dot_general(a.astype(BF16), b.astype(BF16), (((1,), (1,)), ((), ())),
                           preferred_element_type=F32)


def _dot_tn(a, b):
    return lax.dot_general(a.astype(BF16), b.astype(BF16), (((0,), (0,)), ((), ())),
                           preferred_element_type=F32)


def _sigmoid(x):
    return 1.0 / (1.0 + jnp.exp(-x))


def _swap_halves(x, half):
    n = x.shape[-1]
    lane = lax.broadcasted_iota(jnp.int32, x.shape, x.ndim - 1)
    first = (lane % (2 * half)) < half
    return jnp.where(first, pltpu.roll(x, n - half, x.ndim - 1), pltpu.roll(x, half, x.ndim - 1))


IN_BLK = 512


def _in_proj_kernel(x_ref, w_ref, g_ref, c_ref, s_ref, o_ref, xb_ref, *, n_raw, n_sig):
    j = pl.program_id(1)

    @pl.when(j == 0)
    def _():
        xb_ref[...] = x_ref[...].astype(BF16)

    z = jnp.dot(xb_ref[...], w_ref[...], preferred_element_type=F32)

    @pl.when(j < n_raw)
    def _():
        o_ref[...] = z

    @pl.when((j >= n_raw) & (j < n_raw + n_sig))
    def _():
        o_ref[...] = _sigmoid(z)

    @pl.when((j >= n_raw + n_sig) & (j < n_raw + n_sig + 2))
    def _():
        ms = jnp.mean(z * z, axis=-1, keepdims=True)
        o_ref[...] = z * lax.rsqrt(ms + RMS_EPS) * g_ref[...]

    @pl.when(j == n_raw + n_sig + 2)
    def _():
        o_ref[...] = z * c_ref[...] + _swap_halves(z, QK_ROPE // 2) * s_ref[...]


def _in_proj(x, w_perm, gains, ctab, stab, *, d_model, tm):
    n = x.shape[0]
    cols = w_perm.shape[1]
    nj = cols // IN_BLK
    n_raw = 3 * d_model // IN_BLK
    n_sig = 2 * d_model // IN_BLK
    assert nj == n_raw + n_sig + 3
    ntab = ctab.shape[0] // tm
    return pl.pallas_call(
        functools.partial(_in_proj_kernel, n_raw=n_raw, n_sig=n_sig),
        grid=(n // tm, nj),
        in_specs=[
            pl.BlockSpec((tm, d_model), lambda i, j: (i, 0)),
            pl.BlockSpec((d_model, IN_BLK), lambda i, j: (0, j)),
            pl.BlockSpec((1, IN_BLK), lambda i, j: (0, j)),
            pl.BlockSpec((tm, IN_BLK), lambda i, j: (i % ntab, 0)),
            pl.BlockSpec((tm, IN_BLK), lambda i, j: (i % ntab, 0)),
        ],
        out_specs=pl.BlockSpec((tm, IN_BLK), lambda i, j: (i, j)),
        out_shape=jax.ShapeDtypeStruct((n, cols), F32),
        scratch_shapes=[pltpu.VMEM((tm, d_model), BF16)],
        compiler_params=_cparams(("arbitrary", "arbitrary")),
        name="in_proj",
    )(x, w_perm, gains, ctab, stab)


def _q_proj_kernel(cq_ref, w_ref, c_ref, s_ref, o_ref, *, n_nope_blocks):
    j = pl.program_id(1)
    z = _dot(cq_ref[...], w_ref[...])

    @pl.when(j < n_nope_blocks)
    def _():
        o_ref[...] = z.astype(o_ref.dtype)

    @pl.when(j >= n_nope_blocks)
    def _():
        o_ref[...] = (z * c_ref[...] + _swap_halves(z, QK_ROPE // 2) * s_ref[...]).astype(o_ref.dtype)


def _q_proj(zp, cq_block, w_uq_perm, ctab, stab, *, tm):
    n = zp.shape[0]
    q_lora = w_uq_perm.shape[0]
    cols = w_uq_perm.shape[1]
    tn = MLA_HEADS * QK_ROPE
    ntab = ctab.shape[0] // tm
    return pl.pallas_call(
        functools.partial(_q_proj_kernel, n_nope_blocks=MLA_HEADS * QK_NOPE // tn),
        grid=(n // tm, cols // tn),
        in_specs=[
            pl.BlockSpec((tm, q_lora), lambda i, j: (i, cq_block)),
            pl.BlockSpec((q_lora, tn), lambda i, j: (0, j)),
            pl.BlockSpec((tm, tn), lambda i, j: (i % ntab, 0)),
            pl.BlockSpec((tm, tn), lambda i, j: (i % ntab, 0)),
        ],
        out_specs=pl.BlockSpec((tm, tn), lambda i, j: (i, j)),
        out_shape=jax.ShapeDtypeStruct((n, cols), BF16),
        compiler_params=_cparams(("arbitrary", "arbitrary")),
        name="q_proj",
    )(zp, w_uq_perm, ctab, stab)


def _mm_kernel(x_ref, w_ref, o_ref):
    o_ref[...] = _dot(x_ref[...], w_ref[...]).astype(o_ref.dtype)


def _mm(x, x_block, k, w, *, tm, tn, out_dtype, name):
    n = x.shape[0]
    cols = w.shape[1]
    return pl.pallas_call(
        _mm_kernel,
        grid=(n // tm, cols // tn),
        in_specs=[
            pl.BlockSpec((tm, k), lambda i, j: (i, x_block)),
            pl.BlockSpec((k, tn), lambda i, j: (0, j)),
        ],
        out_specs=pl.BlockSpec((tm, tn), lambda i, j: (i, j)),
        out_shape=jax.ShapeDtypeStruct((n, cols), out_dtype),
        compiler_params=_cparams(("arbitrary", "arbitrary")),
        name=name,
    )(x, w)


def _head_mm_kernel(x_ref, w_ref, o_ref):
    o_ref[...] = _dot(x_ref[...], w_ref[0]).astype(o_ref.dtype)


def _head_mm(x, w, *, out_dtype, name):
    n = x.shape[0]
    h, k, m = w.shape
    return pl.pallas_call(
        _head_mm_kernel,
        grid=(h,),
        in_specs=[
            pl.BlockSpec((n, k), lambda i: (0, i)),
            pl.BlockSpec((1, k, m), lambda i: (i, 0, 0)),
        ],
        out_specs=pl.BlockSpec((n, m), lambda i: (0, i)),
        out_shape=jax.ShapeDtypeStruct((n, h * m), out_dtype),
        compiler_params=_cparams(("arbitrary",)),
        name=name,
    )(x, w)


HEADS_PER_STEP = 2


def _attn_prompt_kernel(qi_ref, ki_ref, qn_ref, qp_ref, kn_ref, v_ref, kpe_ref, o_ref,
                        m_ref, l_ref, acc_ref, *, tq, tk):
    step = pl.program_id(2)
    qi = qi_ref[step]
    ki = ki_ref[step]

    @pl.when(ki == 0)
    def _():
        m_ref[...] = jnp.full(m_ref.shape, NEG_INF, F32)
        l_ref[...] = jnp.zeros(l_ref.shape, F32)
        acc_ref[...] = jnp.zeros(acc_ref.shape, F32)

    kpe = kpe_ref[...].astype(BF16)
    row = lax.broadcasted_iota(jnp.int32, (tq, tk), 0) + qi * tq
    col = lax.broadcasted_iota(jnp.int32, (tq, tk), 1) + ki * tk
    visible = col <= row
    for h in range(HEADS_PER_STEP):
        qn = qn_ref[:, h * QK_NOPE:(h + 1) * QK_NOPE]
        qp = qp_ref[:, h * QK_ROPE:(h + 1) * QK_ROPE]
        kn = kn_ref[:, h * QK_NOPE:(h + 1) * QK_NOPE]
        s = (_dot_nt(qn, kn) + _dot_nt(qp, kpe)) * SM_SCALE
        s = jnp.where(visible, s, NEG_INF)
        m_prev = m_ref[h]
        m_next = jnp.maximum(m_prev, jnp.max(s, axis=-1, keepdims=True))
        alpha = jnp.exp(m_prev - m_next)
        p = jnp.exp(s - m_next)
        l_ref[h] = alpha * l_ref[h] + jnp.sum(p, axis=-1, keepdims=True)
        acc_ref[h] = alpha * acc_ref[h] + _dot(p, v_ref[:, h * V_HEAD:(h + 1) * V_HEAD])
        m_ref[h] = m_next

    @pl.when(ki == qi)
    def _():
        for h in range(HEADS_PER_STEP):
            o_ref[:, h * V_HEAD:(h + 1) * V_HEAD] = acc_ref[h] / l_ref[h]


def _attn_prompt(q, kv, kpe, *, batch, seq, tq):
    tk = tq
    nq = seq // tq
    qi_tab = jnp.asarray([qi for qi in range(nq) for _ in range(qi + 1)], jnp.int32)
    ki_tab = jnp.asarray([ki for qi in range(nq) for ki in range(qi + 1)], jnp.int32)
    n_pairs = MLA_HEADS // HEADS_PER_STEP
    nope_w = HEADS_PER_STEP * QK_NOPE
    pe_w = HEADS_PER_STEP * QK_ROPE
    v_w = HEADS_PER_STEP * V_HEAD
    pe_off = MLA_HEADS * QK_NOPE // pe_w
    v_off = MLA_HEADS * QK_NOPE // v_w
    grid_spec = pltpu.PrefetchScalarGridSpec(
        num_scalar_prefetch=2,
        grid=(batch, n_pairs, int(qi_tab.shape[0])),
        in_specs=[
            pl.BlockSpec((tq, nope_w), lambda b, hp, s, qt, kt: (b * nq + qt[s], hp)),
            pl.BlockSpec((tq, pe_w), lambda b, hp, s, qt, kt: (b * nq + qt[s], pe_off + hp)),
            pl.BlockSpec((tk, nope_w), lambda b, hp, s, qt, kt: (b * nq + kt[s], hp)),
            pl.BlockSpec((tk, v_w), lambda b, hp, s, qt, kt: (b * nq + kt[s], v_off + hp)),
            pl.BlockSpec((tk, QK_ROPE), lambda b, hp, s, qt, kt: (b * nq + kt[s], 0)),
        ],
        out_specs=pl.BlockSpec((tq, v_w), lambda b, hp, s, qt, kt: (b * nq + qt[s], hp)),
        scratch_shapes=[
            pltpu.VMEM((HEADS_PER_STEP, tq, 1), F32),
            pltpu.VMEM((HEADS_PER_STEP, tq, 1), F32),
            pltpu.VMEM((HEADS_PER_STEP, tq, V_HEAD), F32),
        ],
    )
    return pl.pallas_call(
        functools.partial(_attn_prompt_kernel, tq=tq, tk=tk),
        grid_spec=grid_spec,
        out_shape=jax.ShapeDtypeStruct((batch * seq, MLA_HEADS * V_HEAD), F32),
        compiler_params=_cparams(("arbitrary", "arbitrary", "arbitrary")),
        name="attn_prompt",
    )(qi_tab, ki_tab, q, q, kv, kv, kpe)


def _attn_paged_kernel(pt_ref, ql_ref, qp_ref, cn_ref, kn_ref, *refs, pages, page, t_new):
    ckv_refs = refs[:pages]
    kpe_refs = refs[pages:2 * pages]
    o_ref, m_ref, l_ref, acc_ref = refs[2 * pages:]
    g = pl.program_id(1)
    ng = pl.num_programs(1)

    @pl.when(g == 0)
    def _():
        m_ref[...] = jnp.full(m_ref.shape, NEG_INF, F32)
        l_ref[...] = jnp.zeros(l_ref.shape, F32)
        acc_ref[...] = jnp.zeros(acc_ref.shape, F32)

    ql = ql_ref[0]
    qp = qp_ref[0]
    keys = [r[0].astype(BF16) for r in ckv_refs]
    s = jnp.concatenate(
        [_dot_nt(ql, keys[i]) + _dot_nt(qp, kpe_refs[i][0]) for i in range(pages)], axis=-1) * SM_SCALE
    m_prev = m_ref[...]
    m_next = jnp.maximum(m_prev, jnp.max(s, axis=-1, keepdims=True))
    alpha = jnp.exp(m_prev - m_next)
    p = jnp.exp(s - m_next)
    l_ref[...] = alpha * l_ref[...] + jnp.sum(p, axis=-1, keepdims=True)
    pv = _dot(p[:, 0:page], keys[0])
    for i in range(1, pages):
        pv = pv + _dot(p[:, i * page:(i + 1) * page], keys[i])
    acc_ref[...] = alpha * acc_ref[...] + pv
    m_ref[...] = m_next

    @pl.when(g == ng - 1)
    def _():
        cn = cn_ref[0].astype(BF16)
        rows = ql.shape[0]
        sn = (_dot_nt(ql, cn) + _dot_nt(qp, kn_ref[0])) * SM_SCALE
        t_row = lax.broadcasted_iota(jnp.int32, (rows, t_new), 0) // (rows // t_new)
        t_col = lax.broadcasted_iota(jnp.int32, (rows, t_new), 1)
        sn = jnp.where(t_col <= t_row, sn, NEG_INF)
        m_prev = m_ref[...]
        m_next = jnp.maximum(m_prev, jnp.max(sn, axis=-1, keepdims=True))
        alpha = jnp.exp(m_prev - m_next)
        p = jnp.exp(sn - m_next)
        l = alpha * l_ref[...] + jnp.sum(p, axis=-1, keepdims=True)
        acc = alpha * acc_ref[...] + _dot(p, cn)
        o_ref[0] = (acc / l).astype(o_ref.dtype)


def _attn_paged(page_table, q_lat, q_pe, ckv_new, kpe_new, cache_ckv, cache_kpe, *, pages):
    nseq, rows, kv_lora = q_lat.shape
    t_new = ckv_new.shape[1]
    page = cache_ckv.shape[1]
    n_pages = page_table.shape[1]
    assert n_pages % pages == 0
    in_specs = [
        pl.BlockSpec((1, rows, kv_lora), lambda b, g, pt: (b, 0, 0)),
        pl.BlockSpec((1, rows, QK_ROPE), lambda b, g, pt: (b, 0, 0)),
        pl.BlockSpec((1, t_new, kv_lora), lambda b, g, pt: (b, 0, 0)),
        pl.BlockSpec((1, t_new, QK_ROPE), lambda b, g, pt: (b, 0, 0)),
    ]
    for i in range(pages):
        in_specs.append(pl.BlockSpec(
            (1, page, kv_lora), lambda b, g, pt, i=i: (pt[b, g * pages + i], 0, 0)))
    for i in range(pages):
        in_specs.append(pl.BlockSpec(
            (1, page, QK_ROPE), lambda b, g, pt, i=i: (pt[b, g * pages + i], 0, 0)))
    grid_spec = pltpu.PrefetchScalarGridSpec(
        num_scalar_prefetch=1,
        grid=(nseq, n_pages // pages),
        in_specs=in_specs,
        out_specs=pl.BlockSpec((1, rows, kv_lora), lambda b, g, pt: (b, 0, 0)),
        scratch_shapes=[
            pltpu.VMEM((rows, 1), F32),
            pltpu.VMEM((rows, 1), F32),
            pltpu.VMEM((rows, kv_lora), F32),
        ],
    )
    return pl.pallas_call(
        functools.partial(_attn_paged_kernel, pages=pages, page=page, t_new=t_new),
        grid_spec=grid_spec,
        out_shape=jax.ShapeDtypeStruct((nseq, rows, kv_lora), BF16),
        compiler_params=_cparams(("arbitrary", "arbitrary")),
        name="attn_paged",
    )(page_table, q_lat, q_pe, ckv_new, kpe_new, *([cache_ckv] * pages), *([cache_kpe] * pages))


def _rwkv_kernel(r_ref, k_ref, v_ref, misc_ref, shift_ref, s0_ref,
                 mu_ref, w0_ref, a0_ref, kkw_ref, kaw_ref, rk_ref, gng_ref, gnb_ref,
                 w2_ref, a2_ref, g2_ref,
                 o_ref, sout_ref,
                 prev_ref, st_ref, rt_ref, kt_ref, vv_ref, kkr_ref, aa_ref, wp_ref, wi_ref, bon_ref,
                 y_ref, *, chunk, heads, head_group):
    c = pl.program_id(1)
    nc = pl.num_programs(1)
    d = r_ref.shape[1]
    misc_w = misc_ref.shape[1]
    lora_w = w2_ref.shape[0]

    @pl.when(c == 0)
    def _():
        prev_ref[...] = shift_ref[0]
        st_ref[...] = s0_ref[0]

    row0 = lax.broadcasted_iota(jnp.int32, (chunk, 1), 0) == 0

    def mix(z, lo, width):
        prev = jnp.where(row0, prev_ref[:, lo:lo + width], pltpu.roll(z, 1, 0))
        return z + (prev - z) * mu_ref[:, lo:lo + width]

    r_raw = r_ref[...]
    k_raw = k_ref[...]
    v_raw = v_ref[...]
    m_raw = misc_ref[...]
    r = mix(r_raw, 0, d)
    k = mix(k_raw, d, d)
    v = mix(v_raw, 2 * d, d)
    m = mix(m_raw, 3 * d, misc_w)
    last = chunk - 1
    prev_ref[:, 0:d] = r_raw[last:last + 1]
    prev_ref[:, d:2 * d] = k_raw[last:last + 1]
    prev_ref[:, 2 * d:3 * d] = v_raw[last:last + 1]
    prev_ref[:, 3 * d:3 * d + misc_w] = m_raw[last:last + 1]

    g_lo = m[:, 0:GATE_LORA]
    lo = m[:, GATE_LORA:GATE_LORA + lora_w]
    w_pre = w0_ref[...] + _dot(jnp.tanh(lo), w2_ref[...])
    a = _sigmoid(a0_ref[...] + _dot(lo, a2_ref[...]))
    gate = _dot(_sigmoid(g_lo), g2_ref[...])
    nw = -w_pre
    softplus = jnp.maximum(nw, 0.0) + jnp.log(1.0 + jnp.exp(-jnp.abs(nw)))
    lw = -jnp.exp(-softplus - 0.5)
    ti = lax.broadcasted_iota(jnp.int32, (chunk, chunk), 0)
    tj = lax.broadcasted_iota(jnp.int32, (chunk, chunk), 1)
    incl = ti >= tj
    strict = ti > tj
    tri = incl.astype(BF16)
    hi = lw.astype(BF16)
    mid = (lw - hi.astype(F32)).astype(BF16)
    low = (lw - hi.astype(F32) - mid.astype(F32)).astype(BF16)
    cs = (jnp.dot(tri, hi, preferred_element_type=F32) + jnp.dot(tri, mid, preferred_element_type=F32)
          + jnp.dot(tri, low, preferred_element_type=F32))
    w_inv = jnp.exp(-cs)
    kp = k * (1.0 + (a - 1.0) * kaw_ref[...])
    rt_ref[...] = r * jnp.exp(cs)
    kt_ref[...] = kp * w_inv
    vv_ref[...] = v
    kkr_ref[...] = k * kkw_ref[...]
    aa_ref[...] = a
    wp_ref[...] = jnp.exp(cs - lw)
    wi_ref[...] = w_inv
    bon_ref[...] = r * kp * rk_ref[...]

    n_double = max(1, (chunk - 1).bit_length())
    gw = head_group * RW_HEAD

    def group_body(gi, carry):
        off = pl.multiple_of(gi * gw, gw)
        sl = pl.ds(off, gw)
        rt_g = rt_ref[:, sl]
        kt_g = kt_ref[:, sl]
        v_g = vv_ref[:, sl]
        kkr_g = kkr_ref[:, sl]
        a_g = aa_ref[:, sl]
        wp_g = wp_ref[:, sl]
        wi_g = wi_ref[:, sl]
        bon_g = bon_ref[:, sl]
        gng_g = gng_ref[:, sl]
        gnb_g = gnb_ref[:, sl]
        ys = []
        for i in range(head_group):
            hs = slice(i * RW_HEAD, (i + 1) * RW_HEAD)
            hidx = gi * head_group + i
            s0 = st_ref[hidx]
            kkr = kkr_g[:, hs]
            nrm = jnp.sqrt(jnp.sum(kkr * kkr, axis=-1, keepdims=True))
            kk = kkr / jnp.maximum(nrm, 1e-12)
            at = -kk * wp_g[:, hs]
            bt = kk * a_g[:, hs] * wi_g[:, hs]
            rt = rt_g[:, hs]
            kt = kt_g[:, hs]
            vh = v_g[:, hs]
            lhs = jnp.concatenate([at, rt], axis=0)
            rhs = jnp.concatenate([bt, kt], axis=0)
            gram = _dot_nt(lhs, rhs)
            l_ab = jnp.where(strict, gram[:chunk, :chunk], 0.0)
            l_ak = jnp.where(strict, gram[:chunk, chunk:], 0.0)
            a_rb = jnp.where(incl, gram[chunk:, :chunk], 0.0)
            a_rk = jnp.where(incl, gram[chunk:, chunk:], 0.0)
            h0 = _dot_nt(lhs, s0)
            u = h0[:chunk] + _dot(l_ak, vh)
            pw = l_ab
            for step in range(n_double):
                u = u + _dot(pw, u)
                if step + 1 < n_double:
                    pw = _dot(pw, pw)
            uv = jnp.concatenate([u, vh], axis=0)
            y = h0[chunk:] + _dot(jnp.concatenate([a_rb, a_rk], axis=1), uv)
            w_last = 1.0 / wi_g[last:last + 1, hs]
            st_ref[hidx] = (s0 + _dot_tn(uv, rhs)) * w_last
            mu_y = jnp.mean(y, axis=-1, keepdims=True)
            yc = y - mu_y
            var = jnp.mean(yc * yc, axis=-1, keepdims=True)
            yn = yc * lax.rsqrt(var + GN_EPS) * gng_g[:, hs] + gnb_g[:, hs]
            ys.append(yn + jnp.sum(bon_g[:, hs], axis=-1, keepdims=True) * vh)
        y_ref[:, sl] = jnp.concatenate(ys, axis=-1)
        return carry

    lax.fori_loop(0, heads // head_group, group_body, 0)
    o_ref[...] = y_ref[...] * gate

    @pl.when(c == nc - 1)
    def _():
        sout_ref[0] = st_ref[...]


def _rwkv(zp, misc_block, shift_perm, s0, vecs, w2p, a2p, g2, *, batch, seq, chunk, head_group):
    d = g2.shape[1]
    heads = d // RW_HEAD
    nc = seq // chunk
    misc_w = IN_BLK
    row = lambda b, c: b * nc + c
    full = lambda shape: pl.BlockSpec(shape, lambda b, c: (0,) * len(shape))
    mu, w0, a0, kkw, kaw, rk, gng, gnb = vecs
    big = pltpu.VMEM((chunk, d), F32)
    return pl.pallas_call(
        functools.partial(_rwkv_kernel, chunk=chunk, heads=heads, head_group=head_group),
        grid=(batch, nc),
        in_specs=[
            pl.BlockSpec((chunk, d), lambda b, c: (row(b, c), 0)),
            pl.BlockSpec((chunk, d), lambda b, c: (row(b, c), 1)),
            pl.BlockSpec((chunk, d), lambda b, c: (row(b, c), 2)),
            pl.BlockSpec((chunk, misc_w), lambda b, c: (row(b, c), misc_block)),
            pl.BlockSpec((1, 1, 3 * d + misc_w), lambda b, c: (b, 0, 0)),
            pl.BlockSpec((1, heads, RW_HEAD, RW_HEAD), lambda b, c: (b, 0, 0, 0)),
            full((1, 3 * d + misc_w)),
            full((1, d)), full((1, d)), full((1, d)), full((1, d)), full((1, d)), full((1, d)), full((1, d)),
            full(w2p.shape), full(a2p.shape), full(g2.shape),
        ],
        out_specs=[
            pl.BlockSpec((chunk, d), lambda b, c: (row(b, c), 0)),
            pl.BlockSpec((1, heads, RW_HEAD, RW_HEAD), lambda b, c: (b, 0, 0, 0)),
        ],
        out_shape=[
            jax.ShapeDtypeStruct((batch * seq, d), F32),
            jax.ShapeDtypeStruct((batch, heads, RW_HEAD, RW_HEAD), F32),
        ],
        scratch_shapes=[
            pltpu.VMEM((1, 3 * d + misc_w), F32),
            pltpu.VMEM((heads, RW_HEAD, RW_HEAD), F32),
            big, big, big, big, big, big, big, big, big,
        ],
        compiler_params=_cparams(("arbitrary", "arbitrary")),
        name="rwkv",
    )(zp, zp, zp, zp, shift_perm, s0, mu, w0, a0, kkw, kaw, rk, gng, gnb, w2p, a2p, g2)


def _layer_norm(x, g, b):
    mu = jnp.mean(x, axis=-1, keepdims=True)
    xc = x - mu
    var = jnp.mean(xc * xc, axis=-1, keepdims=True)
    return xc * lax.rsqrt(var + LN_EPS) * g + b


def _out_proj_kernel(gm_ref, gr_ref, om_ref, or_ref, x_ref, w_ref, g_ref, b_ref, o_ref, *, alpha):
    merged = gm_ref[...] * om_ref[...] + gr_ref[...] * or_ref[...]
    y = alpha * x_ref[...] + _dot(merged, w_ref[...])
    o_ref[...] = _layer_norm(y, g_ref[...], b_ref[...])


def _out_proj(zp, o_mla, o_rw, x, w_o, ln_g, ln_b, *, alpha, tm):
    n, d = x.shape
    rows = lambda blk: pl.BlockSpec((tm, d), lambda i: (i, blk))
    const = lambda shape: pl.BlockSpec(shape, lambda i: (0, 0))
    return pl.pallas_call(
        functools.partial(_out_proj_kernel, alpha=alpha),
        grid=(n // tm,),
        in_specs=[rows(3), rows(4), rows(0), rows(0), rows(0), const((d, d)), const((1, d)), const((1, d))],
        out_specs=rows(0),
        out_shape=jax.ShapeDtypeStruct((n, d), F32),
        compiler_params=_cparams(("arbitrary",)),
        name="out_proj",
    )(zp, zp, o_mla, o_rw, x, w_o, ln_g, ln_b)


def _ffn_kernel(h_ref, wg_ref, wu_ref, wd_ref, g_ref, b_ref, o_ref, hb_ref, acc_ref, *, alpha):
    j = pl.program_id(1)

    @pl.when(j == 0)
    def _():
        hb_ref[...] = h_ref[...].astype(BF16)
        acc_ref[...] = jnp.zeros(acc_ref.shape, F32)

    hb = hb_ref[...]
    gate = jnp.dot(hb, wg_ref[...], preferred_element_type=F32)
    up = jnp.dot(hb, wu_ref[...], preferred_element_type=F32)
    acc_ref[...] += _dot(gate * _sigmoid(gate) * up, wd_ref[...])

    @pl.when(j == pl.num_programs(1) - 1)
    def _():
        o_ref[...] = _layer_norm(alpha * h_ref[...] + acc_ref[...], g_ref[...], b_ref[...])


def _ffn(h, wg, wu, wd, ln_g, ln_b, *, alpha, tm, tf):
    n, d = h.shape
    d_ff = wg.shape[1]
    return pl.pallas_call(
        functools.partial(_ffn_kernel, alpha=alpha),
        grid=(n // tm, d_ff // tf),
        in_specs=[
            pl.BlockSpec((tm, d), lambda i, j: (i, 0)),
            pl.BlockSpec((d, tf), lambda i, j: (0, j)),
            pl.BlockSpec((d, tf), lambda i, j: (0, j)),
            pl.BlockSpec((tf, d), lambda i, j: (j, 0)),
            pl.BlockSpec((1, d), lambda i, j: (0, 0)),
            pl.BlockSpec((1, d), lambda i, j: (0, 0)),
        ],
        out_specs=pl.BlockSpec((tm, d), lambda i, j: (i, 0)),
        out_shape=jax.ShapeDtypeStruct((n, d), F32),
        scratch_shapes=[pltpu.VMEM((tm, d), BF16), pltpu.VMEM((tm, d), F32)],
        compiler_params=_cparams(("arbitrary", "arbitrary")),
        name="ffn",
    )(h, wg, wu, wd, ln_g, ln_b)


def _ple_kernel(h_ref, wpg_ref, pe_ref, wpe_ref, o_ref, hb_ref, *, tn):
    j = pl.program_id(1)

    @pl.when(j == 0)
    def _():
        hb_ref[...] = h_ref[...].astype(BF16)

    gate = _sigmoid(jnp.dot(hb_ref[...], wpg_ref[...], preferred_element_type=F32))
    emb = _dot(pe_ref[...], wpe_ref[...])
    o_ref[...] = h_ref[:, pl.ds(pl.multiple_of(j * tn, tn), tn)] + gate * emb


def _ple(h, w_pg, pe, w_pe, *, tm, tn):
    n, d = h.shape
    p = pe.shape[1]
    return pl.pallas_call(
        functools.partial(_ple_kernel, tn=tn),
        grid=(n // tm, d // tn),
        in_specs=[
            pl.BlockSpec((tm, d), lambda i, j: (i, 0)),
            pl.BlockSpec((d, tn), lambda i, j: (0, j)),
            pl.BlockSpec((tm, p), lambda i, j: (i, 0)),
            pl.BlockSpec((p, tn), lambda i, j: (0, j)),
        ],
        out_specs=pl.BlockSpec((tm, tn), lambda i, j: (i, j)),
        out_shape=jax.ShapeDtypeStruct((n, d), F32),
        scratch_shapes=[pltpu.VMEM((tm, d), BF16)],
        compiler_params=_cparams(("arbitrary", "arbitrary")),
        name="ple",
    )(h, w_pg, pe, w_pe)


def _rope_tables(pos):
    half = QK_ROPE // 2
    inv = ROPE_THETA ** (-jnp.arange(half, dtype=F32) / half)
    ang = pos[:, None] * inv[None, :]
    cos = jnp.cos(ang)
    sin = jnp.sin(ang)
    return jnp.concatenate([cos, cos], axis=-1), jnp.concatenate([-sin, sin], axis=-1)


def _tile_rows(t, tm):
    reps = max(1, tm // t.shape[0])
    return jnp.tile(t, (reps, 1)) if reps > 1 else t


def kernel(x_prompt, x_sample, p_prompt, p_sample, cache_ckv, cache_kpe, state_wkv, state_shift, page_table, w_in, mu_shift, g_q, g_kv, w_uq, w_uk, w_uv, rw_w0, rw_w2, rw_a0, rw_a2, rw_g2, rw_kk, rw_ka, rw_rk, gn_g, gn_b, w_o, ln1_g, ln1_b, w_ffn_gate, w_ffn_up, w_ffn_down, ln2_g, ln2_b, w_ple, w_ple_gate):
    depth = w_in.shape[0]
    assert depth == 1, "single-layer trunk"
    bp, tp, d = x_prompt.shape
    bs, ts, _ = x_sample.shape
    q_lora = g_q.shape[1]
    kv_lora = g_kv.shape[1]
    page = cache_ckv.shape[2]
    past_len = page_table.shape[1] * page
    rw_cols = mu_shift.shape[1]
    mla_cols = q_lora + kv_lora + QK_ROPE
    off_gate = mla_cols + rw_cols
    alpha = (2.0 * depth) ** 0.25
    assert d // RW_HEAD * RW_HEAD == d and q_lora == IN_BLK and kv_lora == IN_BLK
    assert GATE_LORA + DECAY_LORA + AAA_LORA + QK_ROPE == IN_BLK

    o_r, o_wlo, o_k, o_v = 0, d, d + DECAY_LORA, 2 * d + DECAY_LORA
    o_alo = 3 * d + DECAY_LORA
    o_glo = o_alo + AAA_LORA

    def rw_perm(t):
        return jnp.concatenate([
            t[..., o_r:o_r + d], t[..., o_k:o_k + d], t[..., o_v:o_v + d],
            t[..., o_glo:o_glo + GATE_LORA], t[..., o_wlo:o_wlo + DECAY_LORA],
            t[..., o_alo:o_alo + AAA_LORA]], axis=-1)

    wi = w_in[0]
    w_rw = rw_perm(wi[:, mla_cols:off_gate])
    w_perm = jnp.concatenate([
        w_rw[:, :3 * d], wi[:, off_gate:], wi[:, :q_lora + kv_lora],
        w_rw[:, 3 * d:], wi[:, q_lora + kv_lora:mla_cols]], axis=-1).astype(BF16)
    cq_block = 5 * d // IN_BLK
    ckv_block = cq_block + 1
    misc_block = cq_block + 2
    ones = jnp.ones((1, IN_BLK), F32)
    gains = jnp.concatenate(
        [jnp.ones((1, 5 * d), F32), g_q[0][None], g_kv[0][None], ones], axis=-1)
    zeros_kpe = jnp.zeros((1, QK_ROPE), F32)
    mu_perm = jnp.concatenate([rw_perm(mu_shift[0])[None], zeros_kpe], axis=-1)

    wq = w_uq[0]
    w_uq_perm = jnp.concatenate([
        wq[:, :, :QK_NOPE].reshape(q_lora, MLA_HEADS * QK_NOPE),
        wq[:, :, QK_NOPE:].reshape(q_lora, MLA_HEADS * QK_ROPE)], axis=-1).astype(BF16)
    w_kv = jnp.concatenate([
        w_uk[0].reshape(kv_lora, MLA_HEADS * QK_NOPE),
        w_uv[0].reshape(kv_lora, MLA_HEADS * V_HEAD)], axis=-1).astype(BF16)
    w_uk_t = jnp.transpose(w_uk[0], (1, 2, 0)).astype(BF16)
    w_uv_t = jnp.transpose(w_uv[0], (1, 0, 2)).astype(BF16)

    lora_w = IN_BLK - GATE_LORA
    w2p = jnp.zeros((lora_w, d), F32).at[:DECAY_LORA].set(rw_w2[0]).astype(BF16)
    a2p = jnp.zeros((lora_w, d), F32).at[DECAY_LORA:DECAY_LORA + AAA_LORA].set(rw_a2[0]).astype(BF16)
    g2 = rw_g2[0].astype(BF16)
    vecs = (mu_perm, rw_w0, rw_a0, rw_kk, rw_ka, rw_rk[0].reshape(1, d), gn_g, gn_b)

    w_o_b = w_o[0].astype(BF16)
    wg_b = w_ffn_gate[0].astype(BF16)
    wu_b = w_ffn_up[0].astype(BF16)
    wd_b = w_ffn_down[0].astype(BF16)
    w_pg_b = w_ple_gate[0].astype(BF16)
    w_pe_b = w_ple[0].astype(BF16)

    def trunk(x3, pe3, pos, shift_prev, wkv_prev, chunk, attend):
        b, t, _ = x3.shape
        n = b * t
        x = x3.reshape(n, d)
        tm = min(1024, n)
        cos64, sin64 = _rope_tables(pos)
        pad = IN_BLK - QK_ROPE
        c_misc = _tile_rows(jnp.pad(cos64, ((0, 0), (pad, 0)), constant_values=1.0), tm)
        s_misc = _tile_rows(jnp.pad(sin64, ((0, 0), (pad, 0))), tm)
        zp = _in_proj(x, w_perm, gains, c_misc, s_misc, d_model=d, tm=tm)
        ckv = zp[:, ckv_block * IN_BLK:(ckv_block + 1) * IN_BLK]
        kpe = zp[:, misc_block * IN_BLK + pad:]
        tmq = min(512, n)
        c_q = _tile_rows(jnp.tile(cos64, (1, MLA_HEADS)), tmq)
        s_q = _tile_rows(jnp.tile(sin64, (1, MLA_HEADS)), tmq)
        q = _q_proj(zp, cq_block, w_uq_perm, c_q, s_q, tm=tmq)
        o_mla = attend(zp, q, ckv, kpe)
        shift_perm = jnp.concatenate(
            [rw_perm(shift_prev), jnp.zeros((b, QK_ROPE), F32)], axis=-1)[:, None, :]
        o_rw, wkv_new = _rwkv(zp, misc_block, shift_perm, wkv_prev, vecs, w2p, a2p, g2,
                              batch=b, seq=t, chunk=chunk, head_group=2)
        h1 = _out_proj(zp, o_mla, o_rw, x, w_o_b, ln1_g, ln1_b, alpha=alpha, tm=min(256, n))
        h2 = _ffn(h1, wg_b, wu_b, wd_b, ln2_g, ln2_b, alpha=alpha, tm=min(512, n), tf=512)
        out = _ple(h2, w_pg_b, pe3.reshape(n, -1), w_pe_b, tm=min(512, n), tn=512)
        last = zp.reshape(b, t, -1)[:, -1]
        rw_last = jnp.concatenate([last[:, :3 * d], last[:, misc_block * IN_BLK:misc_block * IN_BLK + pad]], -1)
        shift_new = jnp.concatenate([
            rw_last[:, 0:d], rw_last[:, 3 * d + GATE_LORA:3 * d + GATE_LORA + DECAY_LORA],
            rw_last[:, d:3 * d], rw_last[:, 3 * d + GATE_LORA + DECAY_LORA:],
            rw_last[:, 3 * d:3 * d + GATE_LORA]], axis=-1)
        return (out.reshape(b, t, d), ckv.reshape(b, t, kv_lora), kpe.reshape(b, t, QK_ROPE),
                shift_new, wkv_new)

    def attend_prompt(zp, q, ckv, kpe):
        n = zp.shape[0]
        kv = _mm(zp, ckv_block, kv_lora, w_kv, tm=min(1024, n), tn=1024, out_dtype=BF16, name="kv_proj")
        return _attn_prompt(q, kv, kpe, batch=bp, seq=tp, tq=min(512, tp))

    cache_c = cache_ckv[0]
    cache_r = cache_kpe[0]

    def attend_sample(zp, q, ckv, kpe):
        n = zp.shape[0]
        nope_cols = MLA_HEADS * QK_NOPE
        q_lat = _head_mm(q[:, :nope_cols], w_uk_t, out_dtype=BF16, name="q_lat")
        q_lat = q_lat.reshape(bs, ts * MLA_HEADS, kv_lora)
        q_pe = q[:, nope_cols:].reshape(bs, ts * MLA_HEADS, QK_ROPE)
        o_lat = _attn_paged(page_table, q_lat, q_pe, ckv.reshape(bs, ts, kv_lora),
                            kpe.reshape(bs, ts, QK_ROPE), cache_c, cache_r,
                            pages=min(16, page_table.shape[1]))
        return _head_mm(o_lat.reshape(n, MLA_HEADS * kv_lora), w_uv_t, out_dtype=F32, name="o_lat")

    pos_p = jnp.arange(tp, dtype=F32)
    pos_s = past_len + jnp.arange(ts, dtype=F32)
    shift0 = jnp.zeros((bp, rw_cols), F32)
    wkv0 = jnp.zeros((bp, d // RW_HEAD, RW_HEAD, RW_HEAD), F32)
    hp, c1, k1, s1, w1 = trunk(x_prompt, p_prompt[0], pos_p, shift0, wkv0, min(PROMPT_CHUNK, tp),
                               attend_prompt)
    hs, c2, k2, s2, w2 = trunk(x_sample, p_sample[0], pos_s, state_shift[0], state_wkv[0], ts,
                               attend_sample)
    return (hp, hs, c1[None], k1[None], w1[None], s1[None], c2[None], k2[None], w2[None], s2[None])
```

```python
import functools

import jax
import jax.numpy as jnp
from jax import lax
from jax.experimental import pallas as pl
from jax.experimental.pallas import tpu as pltpu

F32 = jnp.float32
BF16 = jnp.bfloat16

MLA_HEADS = 16
QK_NOPE = 128
QK_ROPE = 64
V_HEAD = 128
RW_HEAD = 64
DECAY_LORA = 96
AAA_LORA = 96
GATE_LORA = 256
ROPE_THETA = 10000.0
SM_SCALE = (QK_NOPE + QK_ROPE) ** -0.5
GN_EPS = 64e-5
LN_EPS = 1e-5
RMS_EPS = 1e-6
PROMPT_CHUNK = 64
RW_GROUP_PROMPT = 16
RW_GROUP_SAMPLE = 32

LANES = 128
VMEM_LIMIT_BYTES = 56 * 1024 * 1024

NEG_INF = float("-inf")


def _cparams(sem, flags=None):
    return pltpu.CompilerParams(dimension_semantics=sem, vmem_limit_bytes=VMEM_LIMIT_BYTES, flags=flags)


def _dot(a, b):
    return jnp.dot(a.astype(BF16), b.astype(BF16), preferred_element_type=F32)


def _dot_nt(a, b):
    return lax.dot_general(a.astype(BF16), b.astype(BF16), (((1,), (1,)), ((), ())),
                           preferred_element_type=F32)


def _dot_tn(a, b):
    return lax.dot_general(a.astype(BF16), b.astype(BF16), (((0,), (0,)), ((), ())),
                           preferred_element_type=F32)


def _sigmoid(x):
    return 1.0 / (1.0 + jnp.exp(-x))


def _swap_halves(x, half):
    n = x.shape[-1]
    lane = lax.broadcasted_iota(jnp.int32, x.shape, x.ndim - 1)
    first = (lane % (2 * half)) < half
    return jnp.where(first, pltpu.roll(x, n - half, x.ndim - 1), pltpu.roll(x, half, x.ndim - 1))


IN_BLK = 512


def _in_proj_kernel(x_ref, w_ref, g_ref, c_ref, s_ref, o_ref, xb_ref, *, n_raw, n_sig):
    j = pl.program_id(1)

    @pl.when(j == 0)
    def _():
        xb_ref[...] = x_ref[...].astype(BF16)

    z = jnp.dot(xb_ref[...], w_ref[...], preferred_element_type=F32)

    @pl.when(j < n_raw)
    def _():
        o_ref[...] = z

    @pl.when((j >= n_raw) & (j < n_raw + n_sig))
    def _():
        o_ref[...] = _sigmoid(z)

    @pl.when((j >= n_raw + n_sig) & (j < n_raw + n_sig + 2))
    def _():
        ms = jnp.mean(z * z, axis=-1, keepdims=True)
        o_ref[...] = z * lax.rsqrt(ms + RMS_EPS) * g_ref[...]

    @pl.when(j == n_raw + n_sig + 2)
    def _():
        o_ref[...] = z * c_ref[...] + _swap_halves(z, QK_ROPE // 2) * s_ref[...]


def _in_proj(x, w_perm, gains, ctab, stab, *, d_model, tm):
    n = x.shape[0]
    cols = w_perm.shape[1]
    nj = cols // IN_BLK
    n_raw = 3 * d_model // IN_BLK
    n_sig = 2 * d_model // IN_BLK
    assert nj == n_raw + n_sig + 3
    ntab = ctab.shape[0] // tm
    return pl.pallas_call(
        functools.partial(_in_proj_kernel, n_raw=n_raw, n_sig=n_sig),
        grid=(n // tm, nj),
        in_specs=[
            pl.BlockSpec((tm, d_model), lambda i, j: (i, 0)),
            pl.BlockSpec((d_model, IN_BLK), lambda i, j: (0, j)),
            pl.BlockSpec((1, IN_BLK), lambda i, j: (0, j)),
            pl.BlockSpec((tm, IN_BLK), lambda i, j: (i % ntab, 0)),
            pl.BlockSpec((tm, IN_BLK), lambda i, j: (i % ntab, 0)),
        ],
        out_specs=pl.BlockSpec((tm, IN_BLK), lambda i, j: (i, j)),
        out_shape=jax.ShapeDtypeStruct((n, cols), F32),
        scratch_shapes=[pltpu.VMEM((tm, d_model), BF16)],
        compiler_params=_cparams(("arbitrary", "arbitrary")),
        name="in_proj",
    )(x, w_perm, gains, ctab, stab)


def _q_proj_kernel(cq_ref, w_ref, c_ref, s_ref, o_ref, *, n_nope_blocks):
    j = pl.program_id(1)
    z = _dot(cq_ref[...], w_ref[...])

    @pl.when(j < n_nope_blocks)
    def _():
        o_ref[...] = z.astype(o_ref.dtype)

    @pl.when(j >= n_nope_blocks)
    def _():
        o_ref[...] = (z * c_ref[...] + _swap_halves(z, QK_ROPE // 2) * s_ref[...]).astype(o_ref.dtype)


def _q_proj(zp, cq_block, w_uq_perm, ctab, stab, *, tm):
    n = zp.shape[0]
    q_lora = w_uq_perm.shape[0]
    cols = w_uq_perm.shape[1]
    tn = MLA_HEADS * QK_ROPE
    ntab = ctab.shape[0] // tm
    return pl.pallas_call(
        functools.partial(_q_proj_kernel, n_nope_blocks=MLA_HEADS * QK_NOPE // tn),
        grid=(n // tm, cols // tn),
        in_specs=[
            pl.BlockSpec((tm, q_lora), lambda i, j: (i, cq_block)),
            pl.BlockSpec((q_lora, tn), lambda i, j: (0, j)),
            pl.BlockSpec((tm, tn), lambda i, j: (i % ntab, 0)),
            pl.BlockSpec((tm, tn), lambda i, j: (i % ntab, 0)),
        ],
        out_specs=pl.BlockSpec((tm, tn), lambda i, j: (i, j)),
        out_shape=jax.ShapeDtypeStruct((n, cols), BF16),
        compiler_params=_cparams(("arbitrary", "arbitrary")),
        name="q_proj",
    )(zp, w_uq_perm, ctab, stab)


def _mm_kernel(x_ref, w_ref, o_ref):
    o_ref[...] = _dot(x_ref[...], w_ref[...]).astype(o_ref.dtype)


def _mm(x, x_block, k, w, *, tm, tn, out_dtype, name):
    n = x.shape[0]
    cols = w.shape[1]
    return pl.pallas_call(
        _mm_kernel,
        grid=(n // tm, cols // tn),
        in_specs=[
            pl.BlockSpec((tm, k), lambda i, j: (i, x_block)),
            pl.BlockSpec((k, tn), lambda i, j: (0, j)),
        ],
        out_specs=pl.BlockSpec((tm, tn), lambda i, j: (i, j)),
        out_shape=jax.ShapeDtypeStruct((n, cols), out_dtype),
        compiler_params=_cparams(("arbitrary", "arbitrary")),
        name=name,
    )(x, w)


def _head_mm_kernel(x_ref, w_ref, o_ref):
    o_ref[...] = _dot(x_ref[...], w_ref[0]).astype(o_ref.dtype)


def _head_mm(x, w, *, out_dtype, name):
    n = x.shape[0]
    h, k, m = w.shape
    return pl.pallas_call(
        _head_mm_kernel,
        grid=(h,),
        in_specs=[
            pl.BlockSpec((n, k), lambda i: (0, i)),
            pl.BlockSpec((1, k, m), lambda i: (i, 0, 0)),
        ],
        out_specs=pl.BlockSpec((n, m), lambda i: (0, i)),
        out_shape=jax.ShapeDtypeStruct((n, h * m), out_dtype),
        compiler_params=_cparams(("arbitrary",)),
        name=name,
    )(x, w)


HEADS_PER_STEP = 4
ROW_SPLIT = 2


def _softmax_update(s, m_prev, l_prev):
    m_next = jnp.maximum(m_prev, jnp.max(s, axis=-1, keepdims=True))
    alpha = jnp.exp(m_prev - m_next)
    p = jnp.exp(s - m_next)
    l_next = alpha * l_prev + jnp.sum(p, axis=-1, keepdims=True)
    return m_next, l_next, alpha, p.astype(BF16)


def _attn_prompt_kernel(qi_ref, ki_ref, qn_ref, qp_ref, kn_ref, v_ref, kpe_ref, o_ref,
                        m_ref, l_ref, acc_ref, *, tq, tk):
    step = pl.program_id(2)
    qi = qi_ref[step]
    ki = ki_ref[step]
    heads = range(HEADS_PER_STEP)

    @pl.when(ki == 0)
    def _():
        m_ref[...] = jnp.full(m_ref.shape, NEG_INF, F32)
        l_ref[...] = jnp.zeros(l_ref.shape, F32)
        acc_ref[...] = jnp.zeros(acc_ref.shape, F32)

    def block(diagonal):
        kpe = kpe_ref[...].astype(BF16)
        rq = tq // ROW_SPLIT
        units = [(h, r * rq) for h in heads for r in range(ROW_SPLIT)]
        s = [(_dot_nt(qn_ref[r0:r0 + rq, h * QK_NOPE:(h + 1) * QK_NOPE],
                      kn_ref[:, h * QK_NOPE:(h + 1) * QK_NOPE])
              + _dot_nt(qp_ref[r0:r0 + rq, h * QK_ROPE:(h + 1) * QK_ROPE], kpe)) * SM_SCALE
             for h, r0 in units]
        if diagonal:
            s = [jnp.where(lax.broadcasted_iota(jnp.int32, (rq, tk), 1)
                           <= lax.broadcasted_iota(jnp.int32, (rq, tk), 0) + r0, su, NEG_INF)
                 for su, (h, r0) in zip(s, units)]
        upd = [_softmax_update(su, m_ref[h, r0:r0 + rq], l_ref[h, r0:r0 + rq])
               for su, (h, r0) in zip(s, units)]
        for (m_next, l_next, alpha, p), (h, r0) in zip(upd, units):
            m_ref[h, r0:r0 + rq] = m_next
            l_ref[h, r0:r0 + rq] = l_next
            acc_ref[h, r0:r0 + rq] = (alpha * acc_ref[h, r0:r0 + rq]
                                      + _dot(p, v_ref[:, h * V_HEAD:(h + 1) * V_HEAD]))

    @pl.when(ki < qi)
    def _():
        block(False)

    @pl.when(ki == qi)
    def _():
        block(True)
        for h in heads:
            o_ref[:, h * V_HEAD:(h + 1) * V_HEAD] = acc_ref[h] / l_ref[h]


def _attn_prompt(q, kv, kpe, *, batch, seq, tq):
    tk = tq
    nq = seq // tq
    qi_tab = jnp.asarray([qi for qi in range(nq) for _ in range(qi + 1)], jnp.int32)
    ki_tab = jnp.asarray([ki for qi in range(nq) for ki in range(qi + 1)], jnp.int32)
    n_pairs = MLA_HEADS // HEADS_PER_STEP
    nope_w = HEADS_PER_STEP * QK_NOPE
    pe_w = HEADS_PER_STEP * QK_ROPE
    v_w = HEADS_PER_STEP * V_HEAD
    pe_off = MLA_HEADS * QK_NOPE // pe_w
    v_off = MLA_HEADS * QK_NOPE // v_w
    grid_spec = pltpu.PrefetchScalarGridSpec(
        num_scalar_prefetch=2,
        grid=(batch, n_pairs, int(qi_tab.shape[0])),
        in_specs=[
            pl.BlockSpec((tq, nope_w), lambda b, hp, s, qt, kt: (b * nq + qt[s], hp)),
            pl.BlockSpec((tq, pe_w), lambda b, hp, s, qt, kt: (b * nq + qt[s], pe_off + hp)),
            pl.BlockSpec((tk, nope_w), lambda b, hp, s, qt, kt: (b * nq + kt[s], hp)),
            pl.BlockSpec((tk, v_w), lambda b, hp, s, qt, kt: (b * nq + kt[s], v_off + hp)),
            pl.BlockSpec((tk, QK_ROPE), lambda b, hp, s, qt, kt: (b * nq + kt[s], 0)),
        ],
        out_specs=pl.BlockSpec((tq, v_w), lambda b, hp, s, qt, kt: (b * nq + qt[s], hp)),
        scratch_shapes=[
            pltpu.VMEM((HEADS_PER_STEP, tq, 1), F32),
            pltpu.VMEM((HEADS_PER_STEP, tq, 1), F32),
            pltpu.VMEM((HEADS_PER_STEP, tq, V_HEAD), F32),
        ],
    )
    return pl.pallas_call(
        functools.partial(_attn_prompt_kernel, tq=tq, tk=tk),
        grid_spec=grid_spec,
        out_shape=jax.ShapeDtypeStruct((batch * seq, MLA_HEADS * V_HEAD), F32),
        compiler_params=_cparams(("arbitrary", "arbitrary", "arbitrary")),
        name="attn_prompt",
    )(qi_tab, ki_tab, q, q, kv, kv, kpe)


def _attn_paged_kernel(pt_ref, ql_ref, qp_ref, cn_ref, kn_ref, ckv_hbm, kpet_hbm, o_ref,
                       kbuf, pbuf, sem, m_ref, l_ref, acc_ref, *, pages, page, t_new):
    b = pl.program_id(0)
    g = pl.program_id(1)
    nb = pl.num_programs(0)
    ng = pl.num_programs(1)
    lin = b * ng + g
    slot = lin % 2
    wrap_g = g + 1 == ng
    g_next = jnp.where(wrap_g, 0, g + 1)
    b_next = jnp.where(wrap_g, jnp.where(b + 1 == nb, 0, b + 1), b)

    def ckv_copy(bb, gg, sl, i):
        return pltpu.make_async_copy(ckv_hbm.at[pt_ref[bb, gg * pages + i]], kbuf.at[sl, i], sem.at[0, sl])

    def kpe_copy(bb, gg, sl, i):
        return pltpu.make_async_copy(kpet_hbm.at[pt_ref[bb, gg * pages + i]], pbuf.at[sl, i], sem.at[1, sl])

    @pl.when(lin == 0)
    def _():
        for i in range(pages):
            ckv_copy(b, g, slot, i).start()
            kpe_copy(b, g, slot, i).start()

    @pl.when(g == 0)
    def _():
        m_ref[...] = jnp.full(m_ref.shape, NEG_INF, F32)
        l_ref[...] = jnp.zeros(l_ref.shape, F32)
        acc_ref[...] = jnp.zeros(acc_ref.shape, F32)

    for i in range(pages):
        ckv_copy(b, g, slot, i).wait()
        kpe_copy(b, g, slot, i).wait()

    ql = ql_ref[0]
    qp = qp_ref[0]
    keys, s = [], []
    for i in range(pages):
        ckv_copy(b_next, g_next, 1 - slot, i).start()
        kpe_copy(b_next, g_next, 1 - slot, i).start()
        keys.append(kbuf[slot, i].astype(BF16))
        s.append(_dot_nt(ql, keys[i]) + _dot(qp, pbuf[slot, i]))
    half = pages // 2
    m_run, l_run, acc = m_ref[...], l_ref[...], acc_ref[...]
    for lo, hi in ((0, half), (half, pages)):
        s_part = jnp.concatenate(s[lo:hi], axis=-1) * SM_SCALE
        m_run, l_run, alpha, p = _softmax_update(s_part, m_run, l_run)
        pv = _dot(p[:, 0:page], keys[lo])
        for i in range(lo + 1, hi):
            pv = pv + _dot(p[:, (i - lo) * page:(i - lo + 1) * page], keys[i])
        acc = alpha * acc + pv
    m_ref[...] = m_run
    l_ref[...] = l_run
    acc_ref[...] = acc

    @pl.when(lin == nb * ng - 1)
    def _():
        for i in range(pages):
            ckv_copy(b_next, g_next, 1 - slot, i).wait()
            kpe_copy(b_next, g_next, 1 - slot, i).wait()

    @pl.when(g == ng - 1)
    def _():
        cn = cn_ref[0].astype(BF16)
        rows = ql.shape[0]
        sn = (_dot_nt(ql, cn) + _dot_nt(qp, kn_ref[0])) * SM_SCALE
        t_row = lax.broadcasted_iota(jnp.int32, (rows, t_new), 0) // (rows // t_new)
        t_col = lax.broadcasted_iota(jnp.int32, (rows, t_new), 1)
        sn = jnp.where(t_col <= t_row, sn, NEG_INF)
        m_prev = m_ref[...]
        m_next = jnp.maximum(m_prev, jnp.max(sn, axis=-1, keepdims=True))
        alpha = jnp.exp(m_prev - m_next)
        p = jnp.exp(sn - m_next)
        l = alpha * l_ref[...] + jnp.sum(p, axis=-1, keepdims=True)
        acc = alpha * acc_ref[...] + _dot(p, cn)
        o_ref[0] = (acc / l).astype(o_ref.dtype)


def _attn_paged(page_table, q_lat, q_pe, ckv_new, kpe_new, cache_ckv, cache_kpe_t, *, pages):
    nseq, rows, kv_lora = q_lat.shape
    t_new = ckv_new.shape[1]
    page = cache_ckv.shape[1]
    n_pages = page_table.shape[1]
    assert n_pages % pages == 0 and pages % 2 == 0
    in_specs = [
        pl.BlockSpec((1, rows, kv_lora), lambda b, g, pt: (b, 0, 0)),
        pl.BlockSpec((1, rows, QK_ROPE), lambda b, g, pt: (b, 0, 0)),
        pl.BlockSpec((1, t_new, kv_lora), lambda b, g, pt: (b, 0, 0)),
        pl.BlockSpec((1, t_new, QK_ROPE), lambda b, g, pt: (b, 0, 0)),
        pl.BlockSpec(memory_space=pl.ANY),
        pl.BlockSpec(memory_space=pl.ANY),
    ]
    grid_spec = pltpu.PrefetchScalarGridSpec(
        num_scalar_prefetch=1,
        grid=(nseq, n_pages // pages),
        in_specs=in_specs,
        out_specs=pl.BlockSpec((1, rows, kv_lora), lambda b, g, pt: (b, 0, 0)),
        scratch_shapes=[
            pltpu.VMEM((2, pages, page, kv_lora), F32),
            pltpu.VMEM((2, pages, QK_ROPE, page), F32),
            pltpu.SemaphoreType.DMA((2, 2)),
            pltpu.VMEM((rows, 1), F32),
            pltpu.VMEM((rows, 1), F32),
            pltpu.VMEM((rows, kv_lora), F32),
        ],
    )
    return pl.pallas_call(
        functools.partial(_attn_paged_kernel, pages=pages, page=page, t_new=t_new),
        grid_spec=grid_spec,
        out_shape=jax.ShapeDtypeStruct((nseq, rows, kv_lora), BF16),
        compiler_params=_cparams(("arbitrary", "arbitrary")),
        name="attn_paged",
    )(page_table, q_lat, q_pe, ckv_new, kpe_new, cache_ckv, cache_kpe_t)


def _rwkv_kernel(r_ref, k_ref, v_ref, misc_ref, shift_ref, s0_ref,
                 mu_ref, w0_ref, a0_ref, kkw_ref, kaw_ref, rk_ref, gng_ref, gnb_ref,
                 w2_ref, a2_ref, g2_ref,
                 o_ref, sout_ref,
                 prev_ref, st_ref, rt_ref, kt_ref, vv_ref, kkr_ref, aa_ref, wp_ref, wi_ref, bon_ref,
                 y_ref, *, chunk, heads, head_group):
    c = pl.program_id(1)
    nc = pl.num_programs(1)
    d = r_ref.shape[1]
    misc_w = misc_ref.shape[1]
    lora_w = w2_ref.shape[0]

    @pl.when(c == 0)
    def _():
        prev_ref[...] = shift_ref[0]
        st_ref[...] = s0_ref[0]

    row0 = lax.broadcasted_iota(jnp.int32, (chunk, 1), 0) == 0

    def mix(z, lo, width):
        prev = jnp.where(row0, prev_ref[:, lo:lo + width], pltpu.roll(z, 1, 0))
        return z + (prev - z) * mu_ref[:, lo:lo + width]

    r_raw = r_ref[...]
    k_raw = k_ref[...]
    v_raw = v_ref[...]
    m_raw = misc_ref[...]
    r = mix(r_raw, 0, d)
    k = mix(k_raw, d, d)
    v = mix(v_raw, 2 * d, d)
    m = mix(m_raw, 3 * d, misc_w)
    last = chunk - 1
    prev_ref[:, 0:d] = r_raw[last:last + 1]
    prev_ref[:, d:2 * d] = k_raw[last:last + 1]
    prev_ref[:, 2 * d:3 * d] = v_raw[last:last + 1]
    prev_ref[:, 3 * d:3 * d + misc_w] = m_raw[last:last + 1]

    g_lo = m[:, 0:GATE_LORA]
    lo = m[:, GATE_LORA:GATE_LORA + lora_w]
    w_pre = w0_ref[...] + _dot(jnp.tanh(lo), w2_ref[...])
    a = _sigmoid(a0_ref[...] + _dot(lo, a2_ref[...]))
    gate = _dot(_sigmoid(g_lo), g2_ref[...])
    nw = -w_pre
    softplus = jnp.maximum(nw, 0.0) + jnp.log(1.0 + jnp.exp(-jnp.abs(nw)))
    lw = -jnp.exp(-softplus - 0.5)
    ti = lax.broadcasted_iota(jnp.int32, (chunk, chunk), 0)
    tj = lax.broadcasted_iota(jnp.int32, (chunk, chunk), 1)
    incl = ti >= tj
    strict = ti > tj
    tri = incl.astype(BF16)
    hi = lw.astype(BF16)
    mid = (lw - hi.astype(F32)).astype(BF16)
    low = (lw - hi.astype(F32) - mid.astype(F32)).astype(BF16)
    cs = (jnp.dot(tri, hi, preferred_element_type=F32) + jnp.dot(tri, mid, preferred_element_type=F32)
          + jnp.dot(tri, low, preferred_element_type=F32))
    w_inv = jnp.exp(-cs)
    kp = k * (1.0 + (a - 1.0) * kaw_ref[...])
    rt_ref[...] = r * jnp.exp(cs)
    kt_ref[...] = kp * w_inv
    vv_ref[...] = v
    kkr_ref[...] = k * kkw_ref[...]
    aa_ref[...] = a
    wp_ref[...] = jnp.exp(cs - lw)
    wi_ref[...] = w_inv
    bon_ref[...] = r * kp * rk_ref[...]

    n_double = max(1, (chunk - 1).bit_length())
    gw = head_group * RW_HEAD

    def group_body(gi, carry):
        off = pl.multiple_of(gi * gw, gw)
        sl = pl.ds(off, gw)
        rt_g = rt_ref[:, sl]
        kt_g = kt_ref[:, sl]
        v_g = vv_ref[:, sl]
        kkr_g = kkr_ref[:, sl]
        a_g = aa_ref[:, sl]
        wp_g = wp_ref[:, sl]
        wi_g = wi_ref[:, sl]
        bon_g = bon_ref[:, sl]
        gng_g = gng_ref[:, sl]
        gnb_g = gnb_ref[:, sl]
        s0s = [st_ref[gi * head_group + i] for i in range(head_group)]
        hrange = range(head_group)
        hsl = [slice(i * RW_HEAD, (i + 1) * RW_HEAD) for i in hrange]
        vh = [v_g[:, hs] for hs in hsl]
        lhs, rhs = [], []
        for i, hs in enumerate(hsl):
            kkr = kkr_g[:, hs]
            nrm = jnp.sqrt(jnp.sum(kkr * kkr, axis=-1, keepdims=True))
            kk = kkr / jnp.maximum(nrm, 1e-12)
            at = -kk * wp_g[:, hs]
            bt = kk * a_g[:, hs] * wi_g[:, hs]
            lhs.append(jnp.concatenate([at, rt_g[:, hs]], axis=0))
            rhs.append(jnp.concatenate([bt, kt_g[:, hs]], axis=0))
        gram = [_dot_nt(lhs[i], rhs[i]) for i in hrange]
        h0 = [_dot_nt(lhs[i], s0s[i]) for i in hrange]
        pw = [jnp.where(strict, gram[i][:chunk, :chunk], 0.0) for i in hrange]
        u = [h0[i][:chunk] + _dot(jnp.where(strict, gram[i][:chunk, chunk:], 0.0), vh[i]) for i in hrange]
        for step in range(n_double):
            u = [u[i] + _dot(pw[i], u[i]) for i in hrange]
            if step + 1 < n_double:
                pw = [_dot(pw[i], pw[i]) for i in hrange]
        uv = [jnp.concatenate([u[i], vh[i]], axis=0) for i in hrange]
        ys = []
        for i, hs in enumerate(hsl):
            a_r = jnp.concatenate([jnp.where(incl, gram[i][chunk:, :chunk], 0.0),
                                   jnp.where(incl, gram[i][chunk:, chunk:], 0.0)], axis=1)
            y = h0[i][chunk:] + _dot(a_r, uv[i])
            mu_y = jnp.mean(y, axis=-1, keepdims=True)
            yc = y - mu_y
            var = jnp.mean(yc * yc, axis=-1, keepdims=True)
            yn = yc * lax.rsqrt(var + GN_EPS) * gng_g[:, hs] + gnb_g[:, hs]
            ys.append(yn + jnp.sum(bon_g[:, hs], axis=-1, keepdims=True) * vh[i])
        s_new = [(s0s[i] + _dot_tn(uv[i], rhs[i])) * (1.0 / wi_g[last:last + 1, hsl[i]]) for i in hrange]
        y_ref[:, sl] = jnp.concatenate(ys, axis=-1)
        for i in range(head_group):
            st_ref[gi * head_group + i] = s_new[i]
        return carry

    lax.fori_loop(0, heads // head_group, group_body, 0)
    o_ref[...] = y_ref[...] * gate

    @pl.when(c == nc - 1)
    def _():
        sout_ref[0] = st_ref[...]


def _rwkv(zp, misc_block, shift_perm, s0, vecs, w2p, a2p, g2, *, batch, seq, chunk, head_group):
    d = g2.shape[1]
    heads = d // RW_HEAD
    nc = seq // chunk
    misc_w = IN_BLK
    row = lambda b, c: b * nc + c
    full = lambda shape: pl.BlockSpec(shape, lambda b, c: (0,) * len(shape))
    mu, w0, a0, kkw, kaw, rk, gng, gnb = vecs
    big = pltpu.VMEM((chunk, d), F32)
    return pl.pallas_call(
        functools.partial(_rwkv_kernel, chunk=chunk, heads=heads, head_group=head_group),
        grid=(batch, nc),
        in_specs=[
            pl.BlockSpec((chunk, d), lambda b, c: (row(b, c), 0)),
            pl.BlockSpec((chunk, d), lambda b, c: (row(b, c), 1)),
            pl.BlockSpec((chunk, d), lambda b, c: (row(b, c), 2)),
            pl.BlockSpec((chunk, misc_w), lambda b, c: (row(b, c), misc_block)),
            pl.BlockSpec((1, 1, 3 * d + misc_w), lambda b, c: (b, 0, 0)),
            pl.BlockSpec((1, heads, RW_HEAD, RW_HEAD), lambda b, c: (b, 0, 0, 0)),
            full((1, 3 * d + misc_w)),
            full((1, d)), full((1, d)), full((1, d)), full((1, d)), full((1, d)), full((1, d)), full((1, d)),
            full(w2p.shape), full(a2p.shape), full(g2.shape),
        ],
        out_specs=[
            pl.BlockSpec((chunk, d), lambda b, c: (row(b, c), 0)),
            pl.BlockSpec((1, heads, RW_HEAD, RW_HEAD), lambda b, c: (b, 0, 0, 0)),
        ],
        out_shape=[
            jax.ShapeDtypeStruct((batch * seq, d), F32),
            jax.ShapeDtypeStruct((batch, heads, RW_HEAD, RW_HEAD), F32),
        ],
        scratch_shapes=[
            pltpu.VMEM((1, 3 * d + misc_w), F32),
            pltpu.VMEM((heads, RW_HEAD, RW_HEAD), F32),
            big, big, big, big, big, big, big, big, big,
        ],
        compiler_params=_cparams(("arbitrary", "arbitrary")),
        name="rwkv",
    )(zp, zp, zp, zp, shift_perm, s0, mu, w0, a0, kkw, kaw, rk, gng, gnb, w2p, a2p, g2)


def _layer_norm(x, g, b):
    mu = jnp.mean(x, axis=-1, keepdims=True)
    xc = x - mu
    var = jnp.mean(xc * xc, axis=-1, keepdims=True)
    return xc * lax.rsqrt(var + LN_EPS) * g + b


def _out_proj_kernel(gm_ref, gr_ref, om_ref, or_ref, x_ref, w_ref, g_ref, b_ref, o_ref, *, alpha):
    merged = gm_ref[...] * om_ref[...] + gr_ref[...] * or_ref[...]
    y = alpha * x_ref[...] + _dot(merged, w_ref[...])
    o_ref[...] = _layer_norm(y, g_ref[...], b_ref[...])


def _out_proj(zp, o_mla, o_rw, x, w_o, ln_g, ln_b, *, alpha, tm):
    n, d = x.shape
    rows = lambda blk: pl.BlockSpec((tm, d), lambda i: (i, blk))
    const = lambda shape: pl.BlockSpec(shape, lambda i: (0, 0))
    return pl.pallas_call(
        functools.partial(_out_proj_kernel, alpha=alpha),
        grid=(n // tm,),
        in_specs=[rows(3), rows(4), rows(0), rows(0), rows(0), const((d, d)), const((1, d)), const((1, d))],
        out_specs=rows(0),
        out_shape=jax.ShapeDtypeStruct((n, d), F32),
        compiler_params=_cparams(("arbitrary",)),
        name="out_proj",
    )(zp, zp, o_mla, o_rw, x, w_o, ln_g, ln_b)


def _ffn_kernel(h_ref, wg_ref, wu_ref, wd_ref, g_ref, b_ref, o_ref, hb_ref, acc_ref, *, alpha):
    j = pl.program_id(1)

    @pl.when(j == 0)
    def _():
        hb_ref[...] = h_ref[...].astype(BF16)
        acc_ref[...] = jnp.zeros(acc_ref.shape, F32)

    hb = hb_ref[...]
    gate = jnp.dot(hb, wg_ref[...], preferred_element_type=F32)
    up = jnp.dot(hb, wu_ref[...], preferred_element_type=F32)
    acc_ref[...] += _dot(gate * _sigmoid(gate) * up, wd_ref[...])

    @pl.when(j == pl.num_programs(1) - 1)
    def _():
        o_ref[...] = _layer_norm(alpha * h_ref[...] + acc_ref[...], g_ref[...], b_ref[...])


def _ffn(h, wg, wu, wd, ln_g, ln_b, *, alpha, tm, tf):
    n, d = h.shape
    d_ff = wg.shape[1]
    return pl.pallas_call(
        functools.partial(_ffn_kernel, alpha=alpha),
        grid=(n // tm, d_ff // tf),
        in_specs=[
            pl.BlockSpec((tm, d), lambda i, j: (i, 0)),
            pl.BlockSpec((d, tf), lambda i, j: (0, j)),
            pl.BlockSpec((d, tf), lambda i, j: (0, j)),
            pl.BlockSpec((tf, d), lambda i, j: (j, 0)),
            pl.BlockSpec((1, d), lambda i, j: (0, 0)),
            pl.BlockSpec((1, d), lambda i, j: (0, 0)),
        ],
        out_specs=pl.BlockSpec((tm, d), lambda i, j: (i, 0)),
        out_shape=jax.ShapeDtypeStruct((n, d), F32),
        scratch_shapes=[pltpu.VMEM((tm, d), BF16), pltpu.VMEM((tm, d), F32)],
        compiler_params=_cparams(("arbitrary", "arbitrary")),
        name="ffn",
    )(h, wg, wu, wd, ln_g, ln_b)


def _ple_kernel(h_ref, wpg_ref, pe_ref, wpe_ref, o_ref, hb_ref, *, tn):
    j = pl.program_id(1)

    @pl.when(j == 0)
    def _():
        hb_ref[...] = h_ref[...].astype(BF16)

    gate = _sigmoid(jnp.dot(hb_ref[...], wpg_ref[...], preferred_element_type=F32))
    emb = _dot(pe_ref[...], wpe_ref[...])
    o_ref[...] = h_ref[:, pl.ds(pl.multiple_of(j * tn, tn), tn)] + gate * emb


def _ple(h, w_pg, pe, w_pe, *, tm, tn):
    n, d = h.shape
    p = pe.shape[1]
    return pl.pallas_call(
        functools.partial(_ple_kernel, tn=tn),
        grid=(n // tm, d // tn),
        in_specs=[
            pl.BlockSpec((tm, d), lambda i, j: (i, 0)),
            pl.BlockSpec((d, tn), lambda i, j: (0, j)),
            pl.BlockSpec((tm, p), lambda i, j: (i, 0)),
            pl.BlockSpec((p, tn), lambda i, j: (0, j)),
        ],
        out_specs=pl.BlockSpec((tm, tn), lambda i, j: (i, j)),
        out_shape=jax.ShapeDtypeStruct((n, d), F32),
        scratch_shapes=[pltpu.VMEM((tm, d), BF16)],
        compiler_params=_cparams(("arbitrary", "arbitrary")),
        name="ple",
    )(h, w_pg, pe, w_pe)


def _rope_tables(pos):
    half = QK_ROPE // 2
    inv = ROPE_THETA ** (-jnp.arange(half, dtype=F32) / half)
    ang = pos[:, None] * inv[None, :]
    cos = jnp.cos(ang)
    sin = jnp.sin(ang)
    return jnp.concatenate([cos, cos], axis=-1), jnp.concatenate([-sin, sin], axis=-1)


def _tile_rows(t, tm):
    reps = max(1, tm // t.shape[0])
    return jnp.tile(t, (reps, 1)) if reps > 1 else t


def kernel(x_prompt, x_sample, p_prompt, p_sample, cache_ckv, cache_kpe, state_wkv, state_shift, page_table, w_in, mu_shift, g_q, g_kv, w_uq, w_uk, w_uv, rw_w0, rw_w2, rw_a0, rw_a2, rw_g2, rw_kk, rw_ka, rw_rk, gn_g, gn_b, w_o, ln1_g, ln1_b, w_ffn_gate, w_ffn_up, w_ffn_down, ln2_g, ln2_b, w_ple, w_ple_gate):
    depth = w_in.shape[0]
    assert depth == 1, "single-layer trunk"
    bp, tp, d = x_prompt.shape
    bs, ts, _ = x_sample.shape
    q_lora = g_q.shape[1]
    kv_lora = g_kv.shape[1]
    page = cache_ckv.shape[2]
    past_len = page_table.shape[1] * page
    rw_cols = mu_shift.shape[1]
    mla_cols = q_lora + kv_lora + QK_ROPE
    off_gate = mla_cols + rw_cols
    alpha = (2.0 * depth) ** 0.25
    assert d // RW_HEAD * RW_HEAD == d and q_lora == IN_BLK and kv_lora == IN_BLK
    assert GATE_LORA + DECAY_LORA + AAA_LORA + QK_ROPE == IN_BLK

    o_r, o_wlo, o_k, o_v = 0, d, d + DECAY_LORA, 2 * d + DECAY_LORA
    o_alo = 3 * d + DECAY_LORA
    o_glo = o_alo + AAA_LORA

    def rw_perm(t):
        return jnp.concatenate([
            t[..., o_r:o_r + d], t[..., o_k:o_k + d], t[..., o_v:o_v + d],
            t[..., o_glo:o_glo + GATE_LORA], t[..., o_wlo:o_wlo + DECAY_LORA],
            t[..., o_alo:o_alo + AAA_LORA]], axis=-1)

    wi = w_in[0]
    w_rw = rw_perm(wi[:, mla_cols:off_gate])
    w_perm = jnp.concatenate([
        w_rw[:, :3 * d], wi[:, off_gate:], wi[:, :q_lora + kv_lora],
        w_rw[:, 3 * d:], wi[:, q_lora + kv_lora:mla_cols]], axis=-1).astype(BF16)
    cq_block = 5 * d // IN_BLK
    ckv_block = cq_block + 1
    misc_block = cq_block + 2
    ones = jnp.ones((1, IN_BLK), F32)
    gains = jnp.concatenate(
        [jnp.ones((1, 5 * d), F32), g_q[0][None], g_kv[0][None], ones], axis=-1)
    zeros_kpe = jnp.zeros((1, QK_ROPE), F32)
    mu_perm = jnp.concatenate([rw_perm(mu_shift[0])[None], zeros_kpe], axis=-1)

    wq = w_uq[0]
    w_uq_perm = jnp.concatenate([
        wq[:, :, :QK_NOPE].reshape(q_lora, MLA_HEADS * QK_NOPE),
        wq[:, :, QK_NOPE:].reshape(q_lora, MLA_HEADS * QK_ROPE)], axis=-1).astype(BF16)
    w_kv = jnp.concatenate([
        w_uk[0].reshape(kv_lora, MLA_HEADS * QK_NOPE),
        w_uv[0].reshape(kv_lora, MLA_HEADS * V_HEAD)], axis=-1).astype(BF16)
    w_uk_t = jnp.transpose(w_uk[0], (1, 2, 0)).astype(BF16)
    w_uv_t = jnp.transpose(w_uv[0], (1, 0, 2)).astype(BF16)

    lora_w = IN_BLK - GATE_LORA
    w2p = jnp.zeros((lora_w, d), F32).at[:DECAY_LORA].set(rw_w2[0]).astype(BF16)
    a2p = jnp.zeros((lora_w, d), F32).at[DECAY_LORA:DECAY_LORA + AAA_LORA].set(rw_a2[0]).astype(BF16)
    g2 = rw_g2[0].astype(BF16)
    vecs = (mu_perm, rw_w0, rw_a0, rw_kk, rw_ka, rw_rk[0].reshape(1, d), gn_g, gn_b)

    w_o_b = w_o[0].astype(BF16)
    wg_b = w_ffn_gate[0].astype(BF16)
    wu_b = w_ffn_up[0].astype(BF16)
    wd_b = w_ffn_down[0].astype(BF16)
    w_pg_b = w_ple_gate[0].astype(BF16)
    w_pe_b = w_ple[0].astype(BF16)

    def trunk(x3, pe3, pos, shift_prev, wkv_prev, chunk, head_group, attend):
        b, t, _ = x3.shape
        n = b * t
        x = x3.reshape(n, d)
        tm = min(1024, n)
        cos64, sin64 = _rope_tables(pos)
        pad = IN_BLK - QK_ROPE
        c_misc = _tile_rows(jnp.pad(cos64, ((0, 0), (pad, 0)), constant_values=1.0), tm)
        s_misc = _tile_rows(jnp.pad(sin64, ((0, 0), (pad, 0))), tm)
        zp = _in_proj(x, w_perm, gains, c_misc, s_misc, d_model=d, tm=tm)
        ckv = zp[:, ckv_block * IN_BLK:(ckv_block + 1) * IN_BLK]
        kpe = zp[:, misc_block * IN_BLK + pad:]
        tmq = min(512, n)
        c_q = _tile_rows(jnp.tile(cos64, (1, MLA_HEADS)), tmq)
        s_q = _tile_rows(jnp.tile(sin64, (1, MLA_HEADS)), tmq)
        q = _q_proj(zp, cq_block, w_uq_perm, c_q, s_q, tm=tmq)
        o_mla = attend(zp, q, ckv, kpe)
        shift_perm = jnp.concatenate(
            [rw_perm(shift_prev), jnp.zeros((b, QK_ROPE), F32)], axis=-1)[:, None, :]
        o_rw, wkv_new = _rwkv(zp, misc_block, shift_perm, wkv_prev, vecs, w2p, a2p, g2,
                              batch=b, seq=t, chunk=chunk, head_group=head_group)
        h1 = _out_proj(zp, o_mla, o_rw, x, w_o_b, ln1_g, ln1_b, alpha=alpha, tm=min(256, n))
        h2 = _ffn(h1, wg_b, wu_b, wd_b, ln2_g, ln2_b, alpha=alpha, tm=min(512, n), tf=512)
        out = _ple(h2, w_pg_b, pe3.reshape(n, -1), w_pe_b, tm=min(512, n), tn=512)
        last = zp.reshape(b, t, -1)[:, -1]
        rw_last = jnp.concatenate([last[:, :3 * d], last[:, misc_block * IN_BLK:misc_block * IN_BLK + pad]], -1)
        shift_new = jnp.concatenate([
            rw_last[:, 0:d], rw_last[:, 3 * d + GATE_LORA:3 * d + GATE_LORA + DECAY_LORA],
            rw_last[:, d:3 * d], rw_last[:, 3 * d + GATE_LORA + DECAY_LORA:],
            rw_last[:, 3 * d:3 * d + GATE_LORA]], axis=-1)
        return (out.reshape(b, t, d), ckv.reshape(b, t, kv_lora), kpe.reshape(b, t, QK_ROPE),
                shift_new, wkv_new)

    def attend_prompt(zp, q, ckv, kpe):
        n = zp.shape[0]
        kv = _mm(zp, ckv_block, kv_lora, w_kv, tm=min(1024, n), tn=1024, out_dtype=BF16, name="kv_proj")
        return _attn_prompt(q, kv, kpe, batch=bp, seq=tp, tq=min(512, tp))

    n_phys = cache_ckv.shape[1]
    cache_c = cache_ckv.reshape(n_phys, page, kv_lora)
    cache_r = jnp.swapaxes(cache_kpe.reshape(n_phys, page, QK_ROPE), 1, 2)

    def attend_sample(zp, q, ckv, kpe):
        n = zp.shape[0]
        nope_cols = MLA_HEADS * QK_NOPE
        q_lat = _head_mm(q[:, :nope_cols], w_uk_t, out_dtype=BF16, name="q_lat")
        q_lat = q_lat.reshape(bs, ts * MLA_HEADS, kv_lora)
        q_pe = q[:, nope_cols:].reshape(bs, ts * MLA_HEADS, QK_ROPE)
        o_lat = _attn_paged(page_table, q_lat, q_pe, ckv.reshape(bs, ts, kv_lora),
                            kpe.reshape(bs, ts, QK_ROPE), cache_c, cache_r,
                            pages=min(16, page_table.shape[1]))
        return _head_mm(o_lat.reshape(n, MLA_HEADS * kv_lora), w_uv_t, out_dtype=F32, name="o_lat")

    pos_p = jnp.arange(tp, dtype=F32)
    pos_s = past_len + jnp.arange(ts, dtype=F32)
    shift0 = jnp.zeros((bp, rw_cols), F32)
    wkv0 = jnp.zeros((bp, d // RW_HEAD, RW_HEAD, RW_HEAD), F32)
    hp, c1, k1, s1, w1 = trunk(x_prompt, p_prompt[0], pos_p, shift0, wkv0, min(PROMPT_CHUNK, tp),
                               RW_GROUP_PROMPT, attend_prompt)
    hs, c2, k2, s2, w2 = trunk(x_sample, p_sample[0], pos_s, state_shift[0], state_wkv[0], ts,
                               RW_GROUP_SAMPLE, attend_sample)
    return (hp, hs, c1[None], k1[None], w1[None], s1[None], c2[None], k2[None], w2[None], s2[None])
```

```python
import functools

import jax
import jax.numpy as jnp
from jax import lax
from jax.experimental import pallas as pl
from jax.experimental.pallas import tpu as pltpu

F32 = jnp.float32
BF16 = jnp.bfloat16

MLA_HEADS = 16
QK_NOPE = 128
QK_ROPE = 64
V_HEAD = 128
RW_HEAD = 64
DECAY_LORA = 96
AAA_LORA = 96
GATE_LORA = 256
ROPE_THETA = 10000.0
SM_SCALE = (QK_NOPE + QK_ROPE) ** -0.5
GN_EPS = 64e-5
LN_EPS = 1e-5
RMS_EPS = 1e-6
PROMPT_CHUNK = 64
RW_GROUP_PROMPT = 16
RW_GROUP_SAMPLE = 32

LANES = 128
SEG_BLK = 256
VMEM_LIMIT_BYTES = 56 * 1024 * 1024

NEG_INF = float("-inf")


def _cparams(sem, flags=None):
    return pltpu.CompilerParams(dimension_semantics=sem, vmem_limit_bytes=VMEM_LIMIT_BYTES, flags=flags)


def _dot(a, b):
    return jnp.dot(a.astype(BF16), b.astype(BF16), preferred_element_type=F32)


def _dot_nt(a, b):
    return lax.dot_general(a.astype(BF16), b.astype(BF16), (((1,), (1,)), ((), ())),
                           preferred_element_type=F32)


def _dot_tn(a, b):
    return lax.dot_general(a.astype(BF16), b.astype(BF16), (((0,), (0,)), ((), ())),
                           preferred_element_type=F32)


def _sigmoid(x):
    return 1.0 / (1.0 + jnp.exp(-x))


def _swap_halves(x, half):
    n = x.shape[-1]
    lane = lax.broadcasted_iota(jnp.int32, x.shape, x.ndim - 1)
    first = (lane % (2 * half)) < half
    return jnp.where(first, pltpu.roll(x, n - half, x.ndim - 1), pltpu.roll(x, half, x.ndim - 1))


IN_BLK = 512


def _in_proj_kernel(x_ref, w_ref, g_ref, c_ref, s_ref, o_ref, xb_ref, *, n_raw, n_sig):
    j = pl.program_id(1)

    @pl.when(j == 0)
    def _():
        xb_ref[...] = x_ref[...].astype(BF16)

    z = jnp.dot(xb_ref[...], w_ref[...], preferred_element_type=F32)

    @pl.when(j < n_raw)
    def _():
        o_ref[...] = z

    @pl.when((j >= n_raw) & (j < n_raw + n_sig))
    def _():
        o_ref[...] = _sigmoid(z)

    @pl.when((j >= n_raw + n_sig) & (j < n_raw + n_sig + 2))
    def _():
        ms = jnp.mean(z * z, axis=-1, keepdims=True)
        o_ref[...] = z * lax.rsqrt(ms + RMS_EPS) * g_ref[...]

    @pl.when(j == n_raw + n_sig + 2)
    def _():
        o_ref[...] = z * c_ref[...] + _swap_halves(z, QK_ROPE // 2) * s_ref[...]


def _in_proj(x, w_perm, gains, ctab, stab, *, d_model, tm):
    n = x.shape[0]
    cols = w_perm.shape[1]
    nj = cols // IN_BLK
    n_raw = 3 * d_model // IN_BLK
    n_sig = 2 * d_model // IN_BLK
    assert nj == n_raw + n_sig + 3
    ntab = ctab.shape[0] // tm
    return pl.pallas_call(
        functools.partial(_in_proj_kernel, n_raw=n_raw, n_sig=n_sig),
        grid=(n // tm, nj),
        in_specs=[
            pl.BlockSpec((tm, d_model), lambda i, j: (i, 0)),
            pl.BlockSpec((d_model, IN_BLK), lambda i, j: (0, j)),
            pl.BlockSpec((1, IN_BLK), lambda i, j: (0, j)),
            pl.BlockSpec((tm, IN_BLK), lambda i, j: (i % ntab, 0)),
            pl.BlockSpec((tm, IN_BLK), lambda i, j: (i % ntab, 0)),
        ],
        out_specs=pl.BlockSpec((tm, IN_BLK), lambda i, j: (i, j)),
        out_shape=jax.ShapeDtypeStruct((n, cols), F32),
        scratch_shapes=[pltpu.VMEM((tm, d_model), BF16)],
        compiler_params=_cparams(("arbitrary", "arbitrary")),
        name="in_proj",
    )(x, w_perm, gains, ctab, stab)


def _q_proj_kernel(cq_ref, w_ref, c_ref, s_ref, o_ref, *, n_nope_blocks):
    j = pl.program_id(1)
    z = _dot(cq_ref[...], w_ref[...])

    @pl.when(j < n_nope_blocks)
    def _():
        o_ref[...] = z.astype(o_ref.dtype)

    @pl.when(j >= n_nope_blocks)
    def _():
        o_ref[...] = (z * c_ref[...] + _swap_halves(z, QK_ROPE // 2) * s_ref[...]).astype(o_ref.dtype)


def _q_proj(zp, cq_block, w_uq_perm, ctab, stab, *, tm):
    n = zp.shape[0]
    q_lora = w_uq_perm.shape[0]
    cols = w_uq_perm.shape[1]
    tn = MLA_HEADS * QK_ROPE
    ntab = ctab.shape[0] // tm
    return pl.pallas_call(
        functools.partial(_q_proj_kernel, n_nope_blocks=MLA_HEADS * QK_NOPE // tn),
        grid=(n // tm, cols // tn),
        in_specs=[
            pl.BlockSpec((tm, q_lora), lambda i, j: (i, cq_block)),
            pl.BlockSpec((q_lora, tn), lambda i, j: (0, j)),
            pl.BlockSpec((tm, tn), lambda i, j: (i % ntab, 0)),
            pl.BlockSpec((tm, tn), lambda i, j: (i % ntab, 0)),
        ],
        out_specs=pl.BlockSpec((tm, tn), lambda i, j: (i, j)),
        out_shape=jax.ShapeDtypeStruct((n, cols), BF16),
        compiler_params=_cparams(("arbitrary", "arbitrary")),
        name="q_proj",
    )(zp, w_uq_perm, ctab, stab)


def _mm_kernel(x_ref, w_ref, o_ref):
    o_ref[...] = _dot(x_ref[...], w_ref[...]).astype(o_ref.dtype)


def _mm(x, x_block, k, w, *, tm, tn, out_dtype, name):
    n = x.shape[0]
    cols = w.shape[1]
    return pl.pallas_call(
        _mm_kernel,
        grid=(n // tm, cols // tn),
        in_specs=[
            pl.BlockSpec((tm, k), lambda i, j: (i, x_block)),
            pl.BlockSpec((k, tn), lambda i, j: (0, j)),
        ],
        out_specs=pl.BlockSpec((tm, tn), lambda i, j: (i, j)),
        out_shape=jax.ShapeDtypeStruct((n, cols), out_dtype),
        compiler_params=_cparams(("arbitrary", "arbitrary")),
        name=name,
    )(x, w)


def _head_mm_kernel(x_ref, w_ref, o_ref):
    o_ref[...] = _dot(x_ref[...], w_ref[0]).astype(o_ref.dtype)


def _head_mm(x, w, *, out_dtype, name):
    n = x.shape[0]
    h, k, m = w.shape
    return pl.pallas_call(
        _head_mm_kernel,
        grid=(h,),
        in_specs=[
            pl.BlockSpec((n, k), lambda i: (0, i)),
            pl.BlockSpec((1, k, m), lambda i: (i, 0, 0)),
        ],
        out_specs=pl.BlockSpec((n, m), lambda i: (0, i)),
        out_shape=jax.ShapeDtypeStruct((n, h * m), out_dtype),
        compiler_params=_cparams(("arbitrary",)),
        name=name,
    )(x, w)


HEADS_PER_STEP = 4
ROW_SPLIT = 2


def _softmax_update(s, m_prev, l_prev):
    m_next = jnp.maximum(m_prev, jnp.max(s, axis=-1, keepdims=True))
    alpha = jnp.exp(m_prev - m_next)
    p = jnp.exp(s - m_next)
    l_next = alpha * l_prev + jnp.sum(p, axis=-1, keepdims=True)
    return m_next, l_next, alpha, p.astype(BF16)


def _attn_prompt_kernel(qi_ref, ki_ref, qn_ref, qp_ref, kn_ref, v_ref, kpe_ref, o_ref,
                        m_ref, l_ref, acc_ref, *, tq, tk):
    step = pl.program_id(2)
    qi = qi_ref[step]
    ki = ki_ref[step]
    heads = range(HEADS_PER_STEP)

    @pl.when(ki == 0)
    def _():
        m_ref[...] = jnp.full(m_ref.shape, NEG_INF, F32)
        l_ref[...] = jnp.zeros(l_ref.shape, F32)
        acc_ref[...] = jnp.zeros(acc_ref.shape, F32)

    def block(diagonal):
        kpe = kpe_ref[...].astype(BF16)
        rq = tq // ROW_SPLIT
        units = [(h, r * rq) for h in heads for r in range(ROW_SPLIT)]
        s = [(_dot_nt(qn_ref[r0:r0 + rq, h * QK_NOPE:(h + 1) * QK_NOPE],
                      kn_ref[:, h * QK_NOPE:(h + 1) * QK_NOPE])
              + _dot_nt(qp_ref[r0:r0 + rq, h * QK_ROPE:(h + 1) * QK_ROPE], kpe)) * SM_SCALE
             for h, r0 in units]
        if diagonal:
            s = [jnp.where(lax.broadcasted_iota(jnp.int32, (rq, tk), 1)
                           <= lax.broadcasted_iota(jnp.int32, (rq, tk), 0) + r0, su, NEG_INF)
                 for su, (h, r0) in zip(s, units)]
        upd = [_softmax_update(su, m_ref[h, r0:r0 + rq], l_ref[h, r0:r0 + rq])
               for su, (h, r0) in zip(s, units)]
        for (m_next, l_next, alpha, p), (h, r0) in zip(upd, units):
            m_ref[h, r0:r0 + rq] = m_next
            l_ref[h, r0:r0 + rq] = l_next
            acc_ref[h, r0:r0 + rq] = (alpha * acc_ref[h, r0:r0 + rq]
                                      + _dot(p, v_ref[:, h * V_HEAD:(h + 1) * V_HEAD]))

    @pl.when(ki < qi)
    def _():
        block(False)

    @pl.when(ki == qi)
    def _():
        block(True)
        for h in heads:
            o_ref[:, h * V_HEAD:(h + 1) * V_HEAD] = acc_ref[h] / l_ref[h]


def _attn_prompt(q, kv, kpe, *, batch, seq, tq):
    tk = tq
    nq = seq // tq
    qi_tab = jnp.asarray([qi for qi in range(nq) for _ in range(qi + 1)], jnp.int32)
    ki_tab = jnp.asarray([ki for qi in range(nq) for ki in range(qi + 1)], jnp.int32)
    n_pairs = MLA_HEADS // HEADS_PER_STEP
    nope_w = HEADS_PER_STEP * QK_NOPE
    pe_w = HEADS_PER_STEP * QK_ROPE
    v_w = HEADS_PER_STEP * V_HEAD
    pe_off = MLA_HEADS * QK_NOPE // pe_w
    v_off = MLA_HEADS * QK_NOPE // v_w
    grid_spec = pltpu.PrefetchScalarGridSpec(
        num_scalar_prefetch=2,
        grid=(batch, n_pairs, int(qi_tab.shape[0])),
        in_specs=[
            pl.BlockSpec((tq, nope_w), lambda b, hp, s, qt, kt: (b * nq + qt[s], hp)),
            pl.BlockSpec((tq, pe_w), lambda b, hp, s, qt, kt: (b * nq + qt[s], pe_off + hp)),
            pl.BlockSpec((tk, nope_w), lambda b, hp, s, qt, kt: (b * nq + kt[s], hp)),
            pl.BlockSpec((tk, v_w), lambda b, hp, s, qt, kt: (b * nq + kt[s], v_off + hp)),
            pl.BlockSpec((tk, QK_ROPE), lambda b, hp, s, qt, kt: (b * nq + kt[s], 0)),
        ],
        out_specs=pl.BlockSpec((tq, v_w), lambda b, hp, s, qt, kt: (b * nq + qt[s], hp)),
        scratch_shapes=[
            pltpu.VMEM((HEADS_PER_STEP, tq, 1), F32),
            pltpu.VMEM((HEADS_PER_STEP, tq, 1), F32),
            pltpu.VMEM((HEADS_PER_STEP, tq, V_HEAD), F32),
        ],
    )
    return pl.pallas_call(
        functools.partial(_attn_prompt_kernel, tq=tq, tk=tk),
        grid_spec=grid_spec,
        out_shape=jax.ShapeDtypeStruct((batch * seq, MLA_HEADS * V_HEAD), F32),
        compiler_params=_cparams(("arbitrary", "arbitrary", "arbitrary")),
        name="attn_prompt",
    )(qi_tab, ki_tab, q, q, kv, kv, kpe)


PAGE_SLOTS = 3


def _attn_paged_kernel(pt_ref, ql_ref, qp_ref, cn_ref, kn_ref, ckv_hbm, kpet_hbm, o_ref,
                       kbuf, pbuf, sem, m_ref, l_ref, acc_ref, *, pages, page, t_new):
    b = pl.program_id(0)
    g = pl.program_id(1)
    nb = pl.num_programs(0)
    ng = pl.num_programs(1)
    lin = b * ng + g

    groups = [(b, g, lax.rem(lin, PAGE_SLOTS))]
    for _ in range(PAGE_SLOTS - 1):
        pb, pg, ps = groups[-1]
        wrap = pg + 1 == ng
        groups.append((jnp.where(wrap, jnp.where(pb + 1 == nb, 0, pb + 1), pb),
                       jnp.where(wrap, 0, pg + 1),
                       jnp.where(ps + 1 == PAGE_SLOTS, 0, ps + 1)))

    def copies(ahead):
        bb, gg, sl = groups[ahead]
        out = []
        for i in range(pages):
            pid = pt_ref[bb, gg * pages + i]
            out.append(pltpu.make_async_copy(ckv_hbm.at[pid], kbuf.at[sl, i], sem.at[0, sl]))
            out.append(pltpu.make_async_copy(kpet_hbm.at[pid], pbuf.at[sl, i], sem.at[1, sl]))
        return out

    @pl.when(lin == 0)
    def _():
        for n in range(PAGE_SLOTS - 1):
            for k, c in enumerate(copies(n)):
                c.start(priority=(k // 2) % 2)

    @pl.when(g == 0)
    def _():
        m_ref[...] = jnp.full(m_ref.shape, NEG_INF, F32)
        l_ref[...] = jnp.zeros(l_ref.shape, F32)
        acc_ref[...] = jnp.zeros(acc_ref.shape, F32)

    for c in copies(0):
        c.wait()

    slot = groups[0][2]
    ahead = copies(PAGE_SLOTS - 1)
    ql = ql_ref[0]
    qp = qp_ref[0]
    keys, s = [], []
    for i in range(pages):
        ahead[2 * i].start(priority=i % 2)
        ahead[2 * i + 1].start(priority=i % 2)
        keys.append(kbuf[slot, i].astype(BF16))
        s.append(_dot_nt(ql, keys[i]) + _dot(qp, pbuf[slot, i]))
    half = pages // 2
    m_run, l_run, acc = m_ref[...], l_ref[...], acc_ref[...]
    for lo, hi in ((0, half), (half, pages)):
        s_part = jnp.concatenate(s[lo:hi], axis=-1) * SM_SCALE
        m_run, l_run, alpha, p = _softmax_update(s_part, m_run, l_run)
        pv = _dot(p[:, 0:page], keys[lo])
        for i in range(lo + 1, hi):
            pv = pv + _dot(p[:, (i - lo) * page:(i - lo + 1) * page], keys[i])
        acc = alpha * acc + pv
    m_ref[...] = m_run
    l_ref[...] = l_run
    acc_ref[...] = acc

    @pl.when(lin == nb * ng - 1)
    def _():
        for n in range(1, PAGE_SLOTS):
            for c in copies(n):
                c.wait()

    @pl.when(g == ng - 1)
    def _():
        cn = cn_ref[0].astype(BF16)
        rows = ql.shape[0]
        sn = (_dot_nt(ql, cn) + _dot_nt(qp, kn_ref[0])) * SM_SCALE
        t_row = lax.broadcasted_iota(jnp.int32, (rows, t_new), 0) // (rows // t_new)
        t_col = lax.broadcasted_iota(jnp.int32, (rows, t_new), 1)
        sn = jnp.where(t_col <= t_row, sn, NEG_INF)
        m_prev = m_ref[...]
        m_next = jnp.maximum(m_prev, jnp.max(sn, axis=-1, keepdims=True))
        alpha = jnp.exp(m_prev - m_next)
        p = jnp.exp(sn - m_next)
        l = alpha * l_ref[...] + jnp.sum(p, axis=-1, keepdims=True)
        acc = alpha * acc_ref[...] + _dot(p, cn)
        o_ref[0] = (acc / l).astype(o_ref.dtype)


def _attn_paged(page_table, q_lat, q_pe, ckv_new, kpe_new, cache_ckv, cache_kpe_t, *, pages):
    nseq, rows, kv_lora = q_lat.shape
    t_new = ckv_new.shape[1]
    page = cache_ckv.shape[1]
    n_pages = page_table.shape[1]
    assert n_pages % pages == 0 and pages % 2 == 0
    in_specs = [
        pl.BlockSpec((1, rows, kv_lora), lambda b, g, pt: (b, 0, 0)),
        pl.BlockSpec((1, rows, QK_ROPE), lambda b, g, pt: (b, 0, 0)),
        pl.BlockSpec((1, t_new, kv_lora), lambda b, g, pt: (b, 0, 0)),
        pl.BlockSpec((1, t_new, QK_ROPE), lambda b, g, pt: (b, 0, 0)),
        pl.BlockSpec(memory_space=pl.ANY),
        pl.BlockSpec(memory_space=pl.ANY),
    ]
    grid_spec = pltpu.PrefetchScalarGridSpec(
        num_scalar_prefetch=1,
        grid=(nseq, n_pages // pages),
        in_specs=in_specs,
        out_specs=pl.BlockSpec((1, rows, kv_lora), lambda b, g, pt: (b, 0, 0)),
        scratch_shapes=[
            pltpu.VMEM((PAGE_SLOTS, pages, page, kv_lora), F32),
            pltpu.VMEM((PAGE_SLOTS, pages, QK_ROPE, page), F32),
            pltpu.SemaphoreType.DMA((2, PAGE_SLOTS)),
            pltpu.VMEM((rows, 1), F32),
            pltpu.VMEM((rows, 1), F32),
            pltpu.VMEM((rows, kv_lora), F32),
        ],
    )
    return pl.pallas_call(
        functools.partial(_attn_paged_kernel, pages=pages, page=page, t_new=t_new),
        grid_spec=grid_spec,
        out_shape=jax.ShapeDtypeStruct((nseq, rows, kv_lora), BF16),
        compiler_params=_cparams(("arbitrary", "arbitrary")),
        name="attn_paged",
    )(page_table, q_lat, q_pe, ckv_new, kpe_new, cache_ckv, cache_kpe_t)


def _rwkv_kernel(r_ref, k_ref, v_ref, misc_ref, shift_ref, s0_ref,
                 mu_ref, w0_ref, a0_ref, kkw_ref, kaw_ref, rk_ref, gng_ref, gnb_ref,
                 w2_ref, a2_ref, g2_ref,
                 o_ref, sout_ref,
                 prev_ref, st_ref, wl_ref, rt_ref, kt_ref, vv_ref, at_ref, bt_ref, bon_ref, gate_ref,
                 y_ref, *, chunk, heads, head_group):
    c = pl.program_id(1)
    nc = pl.num_programs(1)
    d = r_ref.shape[1]
    misc_w = misc_ref.shape[1]
    lora_w = w2_ref.shape[0]

    @pl.when(c == 0)
    def _():
        prev_ref[...] = shift_ref[0]
        st_ref[...] = s0_ref[0]

    row0 = lax.broadcasted_iota(jnp.int32, (chunk, 1), 0) == 0

    def mix(z, lo, width):
        prev = jnp.where(row0, prev_ref[:, lo:lo + width], pltpu.roll(z, 1, 0))
        return z + (prev - z) * mu_ref[:, lo:lo + width]

    r_raw = r_ref[...]
    k_raw = k_ref[...]
    v_raw = v_ref[...]
    m_raw = misc_ref[...]
    r = mix(r_raw, 0, d)
    k = mix(k_raw, d, d)
    v = mix(v_raw, 2 * d, d)
    m = mix(m_raw, 3 * d, misc_w)
    last = chunk - 1
    prev_ref[:, 0:d] = r_raw[last:last + 1]
    prev_ref[:, d:2 * d] = k_raw[last:last + 1]
    prev_ref[:, 2 * d:3 * d] = v_raw[last:last + 1]
    prev_ref[:, 3 * d:3 * d + misc_w] = m_raw[last:last + 1]

    g_lo = m[:, 0:GATE_LORA]
    lo = m[:, GATE_LORA:GATE_LORA + lora_w]
    w_pre = w0_ref[...] + _dot(jnp.tanh(lo), w2_ref[...])
    a = _sigmoid(a0_ref[...] + _dot(lo, a2_ref[...]))
    gate = _dot(_sigmoid(g_lo), g2_ref[...])
    nw = -w_pre
    softplus = jnp.maximum(nw, 0.0) + jnp.log(1.0 + jnp.exp(-jnp.abs(nw)))
    lw = -jnp.exp(-softplus - 0.5)
    ti = lax.broadcasted_iota(jnp.int32, (chunk, chunk), 0)
    tj = lax.broadcasted_iota(jnp.int32, (chunk, chunk), 1)
    incl = ti >= tj
    strict = ti > tj
    tri = incl.astype(BF16)
    hi = lw.astype(BF16)
    mid = (lw - hi.astype(F32)).astype(BF16)
    low = (lw - hi.astype(F32) - mid.astype(F32)).astype(BF16)
    cs = (jnp.dot(tri, hi, preferred_element_type=F32) + jnp.dot(tri, mid, preferred_element_type=F32)
          + jnp.dot(tri, low, preferred_element_type=F32))
    gi_ = lax.broadcasted_iota(jnp.int32, (SEG_BLK, SEG_BLK), 0) // RW_HEAD
    gj_ = lax.broadcasted_iota(jnp.int32, (SEG_BLK, SEG_BLK), 1) // RW_HEAD
    seg = (gi_ == gj_).astype(BF16)

    def head_sum(x):
        x_hi = x.astype(BF16).astype(F32)
        x_lo = x - x_hi
        cols = range(0, d, SEG_BLK)
        stacked = jnp.concatenate([x_hi[:, j:j + SEG_BLK] for j in cols]
                                  + [x_lo[:, j:j + SEG_BLK] for j in cols], axis=0)
        sums = jnp.dot(stacked.astype(BF16), seg, preferred_element_type=F32)
        nblk = len(cols)
        return jnp.concatenate(
            [sums[j * chunk:(j + 1) * chunk] + sums[(nblk + j) * chunk:(nblk + j + 1) * chunk]
             for j in range(nblk)], axis=1)

    w_inv = jnp.exp(-cs)
    kp = k * (1.0 + (a - 1.0) * kaw_ref[...])
    kkr = k * kkw_ref[...]
    kk = kkr * lax.rsqrt(jnp.maximum(head_sum(kkr * kkr), 1e-24))
    w_c = jnp.exp(cs)
    rt_ref[...] = r * w_c
    kt_ref[...] = kp * w_inv
    vv_ref[...] = v
    at_ref[...] = -kk * jnp.exp(cs - lw)
    bt_ref[...] = kk * a * w_inv
    wl_ref[...] = w_c[last:last + 1]
    bon_ref[...] = head_sum(r * kp * rk_ref[...])
    gate_ref[...] = gate

    n_double = max(1, (chunk - 1).bit_length())
    gw = head_group * RW_HEAD

    def group_body(gi, carry):
        off = pl.multiple_of(gi * gw, gw)
        sl = pl.ds(off, gw)
        rt_g = rt_ref[:, sl]
        kt_g = kt_ref[:, sl]
        v_g = vv_ref[:, sl]
        at_g = at_ref[:, sl]
        bt_g = bt_ref[:, sl]
        wl_g = wl_ref[:, sl]
        s0s = [st_ref[gi * head_group + i] for i in range(head_group)]
        hrange = range(head_group)
        hsl = [slice(i * RW_HEAD, (i + 1) * RW_HEAD) for i in hrange]
        vh = [v_g[:, hs] for hs in hsl]
        lhs = [jnp.concatenate([at_g[:, hs], rt_g[:, hs]], axis=0) for hs in hsl]
        rhs = [jnp.concatenate([bt_g[:, hs], kt_g[:, hs]], axis=0) for hs in hsl]
        gram = [_dot_nt(lhs[i], rhs[i]) for i in hrange]
        h0 = [_dot_nt(lhs[i], s0s[i]) for i in hrange]
        pw = [jnp.where(strict, gram[i][:chunk, :chunk], 0.0) for i in hrange]
        u = [h0[i][:chunk] + _dot(jnp.where(strict, gram[i][:chunk, chunk:], 0.0), vh[i]) for i in hrange]
        for step in range(n_double):
            u = [u[i] + _dot(pw[i], u[i]) for i in hrange]
            if step + 1 < n_double:
                pw = [_dot(pw[i], pw[i]) for i in hrange]
        uv = [jnp.concatenate([u[i], vh[i]], axis=0) for i in hrange]
        ys = []
        for i in hrange:
            a_r = jnp.concatenate([jnp.where(incl, gram[i][chunk:, :chunk], 0.0),
                                   jnp.where(incl, gram[i][chunk:, chunk:], 0.0)], axis=1)
            ys.append(h0[i][chunk:] + _dot(a_r, uv[i]))
        s_new = [(s0s[i] + _dot_tn(uv[i], rhs[i])) * wl_g[:, hsl[i]] for i in hrange]
        y_ref[:, sl] = jnp.concatenate(ys, axis=-1)
        for i in range(head_group):
            st_ref[gi * head_group + i] = s_new[i]
        return carry

    lax.fori_loop(0, heads // head_group, group_body, 0)
    y = y_ref[...]
    yc = y - head_sum(y) * (1.0 / RW_HEAD)
    var = head_sum(yc * yc) * (1.0 / RW_HEAD)
    yn = yc * lax.rsqrt(var + GN_EPS) * gng_ref[...] + gnb_ref[...]
    o_ref[...] = (yn + bon_ref[...] * vv_ref[...]) * gate_ref[...]

    @pl.when(c == nc - 1)
    def _():
        sout_ref[0] = st_ref[...]


def _rwkv(zp, misc_block, shift_perm, s0, vecs, w2p, a2p, g2, *, batch, seq, chunk, head_group):
    d = g2.shape[1]
    heads = d // RW_HEAD
    nc = seq // chunk
    misc_w = IN_BLK
    row = lambda b, c: b * nc + c
    full = lambda shape: pl.BlockSpec(shape, lambda b, c: (0,) * len(shape))
    mu, w0, a0, kkw, kaw, rk, gng, gnb = vecs
    big = pltpu.VMEM((chunk, d), F32)
    return pl.pallas_call(
        functools.partial(_rwkv_kernel, chunk=chunk, heads=heads, head_group=head_group),
        grid=(batch, nc),
        in_specs=[
            pl.BlockSpec((chunk, d), lambda b, c: (row(b, c), 0)),
            pl.BlockSpec((chunk, d), lambda b, c: (row(b, c), 1)),
            pl.BlockSpec((chunk, d), lambda b, c: (row(b, c), 2)),
            pl.BlockSpec((chunk, misc_w), lambda b, c: (row(b, c), misc_block)),
            pl.BlockSpec((1, 1, 3 * d + misc_w), lambda b, c: (b, 0, 0)),
            pl.BlockSpec((1, heads, RW_HEAD, RW_HEAD), lambda b, c: (b, 0, 0, 0)),
            full((1, 3 * d + misc_w)),
            full((1, d)), full((1, d)), full((1, d)), full((1, d)), full((1, d)), full((1, d)), full((1, d)),
            full(w2p.shape), full(a2p.shape), full(g2.shape),
        ],
        out_specs=[
            pl.BlockSpec((chunk, d), lambda b, c: (row(b, c), 0)),
            pl.BlockSpec((1, heads, RW_HEAD, RW_HEAD), lambda b, c: (b, 0, 0, 0)),
        ],
        out_shape=[
            jax.ShapeDtypeStruct((batch * seq, d), F32),
            jax.ShapeDtypeStruct((batch, heads, RW_HEAD, RW_HEAD), F32),
        ],
        scratch_shapes=[
            pltpu.VMEM((1, 3 * d + misc_w), F32),
            pltpu.VMEM((heads, RW_HEAD, RW_HEAD), F32),
            pltpu.VMEM((1, d), F32),
            big, big, big, big, big, big, big, big,
        ],
        compiler_params=_cparams(("arbitrary", "arbitrary")),
        name="rwkv",
    )(zp, zp, zp, zp, shift_perm, s0, mu, w0, a0, kkw, kaw, rk, gng, gnb, w2p, a2p, g2)


def _layer_norm(x, g, b):
    mu = jnp.mean(x, axis=-1, keepdims=True)
    xc = x - mu
    var = jnp.mean(xc * xc, axis=-1, keepdims=True)
    return xc * lax.rsqrt(var + LN_EPS) * g + b


def _out_proj_kernel(gm_ref, gr_ref, om_ref, or_ref, x_ref, w_ref, g_ref, b_ref, o_ref, *, alpha):
    merged = gm_ref[...] * om_ref[...] + gr_ref[...] * or_ref[...]
    y = alpha * x_ref[...] + _dot(merged, w_ref[...])
    o_ref[...] = _layer_norm(y, g_ref[...], b_ref[...])


def _out_proj(zp, o_mla, o_rw, x, w_o, ln_g, ln_b, *, alpha, tm):
    n, d = x.shape
    rows = lambda blk: pl.BlockSpec((tm, d), lambda i: (i, blk))
    const = lambda shape: pl.BlockSpec(shape, lambda i: (0, 0))
    return pl.pallas_call(
        functools.partial(_out_proj_kernel, alpha=alpha),
        grid=(n // tm,),
        in_specs=[rows(3), rows(4), rows(0), rows(0), rows(0), const((d, d)), const((1, d)), const((1, d))],
        out_specs=rows(0),
        out_shape=jax.ShapeDtypeStruct((n, d), F32),
        compiler_params=_cparams(("arbitrary",)),
        name="out_proj",
    )(zp, zp, o_mla, o_rw, x, w_o, ln_g, ln_b)


def _ffn_kernel(h_ref, wg_ref, wu_ref, wd_ref, g_ref, b_ref, o_ref, hb_ref, acc_ref, *, alpha):
    j = pl.program_id(1)

    @pl.when(j == 0)
    def _():
        hb_ref[...] = h_ref[...].astype(BF16)
        acc_ref[...] = jnp.zeros(acc_ref.shape, F32)

    hb = hb_ref[...]
    gate = jnp.dot(hb, wg_ref[...], preferred_element_type=F32)
    up = jnp.dot(hb, wu_ref[...], preferred_element_type=F32)
    acc_ref[...] += _dot(gate * _sigmoid(gate) * up, wd_ref[...])

    @pl.when(j == pl.num_programs(1) - 1)
    def _():
        o_ref[...] = _layer_norm(alpha * h_ref[...] + acc_ref[...], g_ref[...], b_ref[...])


def _ffn(h, wg, wu, wd, ln_g, ln_b, *, alpha, tm, tf):
    n, d = h.shape
    d_ff = wg.shape[1]
    return pl.pallas_call(
        functools.partial(_ffn_kernel, alpha=alpha),
        grid=(n // tm, d_ff // tf),
        in_specs=[
            pl.BlockSpec((tm, d), lambda i, j: (i, 0)),
            pl.BlockSpec((d, tf), lambda i, j: (0, j)),
            pl.BlockSpec((d, tf), lambda i, j: (0, j)),
            pl.BlockSpec((tf, d), lambda i, j: (j, 0)),
            pl.BlockSpec((1, d), lambda i, j: (0, 0)),
            pl.BlockSpec((1, d), lambda i, j: (0, 0)),
        ],
        out_specs=pl.BlockSpec((tm, d), lambda i, j: (i, 0)),
        out_shape=jax.ShapeDtypeStruct((n, d), F32),
        scratch_shapes=[pltpu.VMEM((tm, d), BF16), pltpu.VMEM((tm, d), F32)],
        compiler_params=_cparams(("arbitrary", "arbitrary")),
        name="ffn",
    )(h, wg, wu, wd, ln_g, ln_b)


def _ple_kernel(h_ref, wpg_ref, pe_ref, wpe_ref, o_ref, hb_ref, *, tn):
    j = pl.program_id(1)

    @pl.when(j == 0)
    def _():
        hb_ref[...] = h_ref[...].astype(BF16)

    gate = _sigmoid(jnp.dot(hb_ref[...], wpg_ref[...], preferred_element_type=F32))
    emb = _dot(pe_ref[...], wpe_ref[...])
    o_ref[...] = h_ref[:, pl.ds(pl.multiple_of(j * tn, tn), tn)] + gate * emb


def _ple(h, w_pg, pe, w_pe, *, tm, tn):
    n, d = h.shape
    p = pe.shape[1]
    return pl.pallas_call(
        functools.partial(_ple_kernel, tn=tn),
        grid=(n // tm, d // tn),
        in_specs=[
            pl.BlockSpec((tm, d), lambda i, j: (i, 0)),
            pl.BlockSpec((d, tn), lambda i, j: (0, j)),
            pl.BlockSpec((tm, p), lambda i, j: (i, 0)),
            pl.BlockSpec((p, tn), lambda i, j: (0, j)),
        ],
        out_specs=pl.BlockSpec((tm, tn), lambda i, j: (i, j)),
        out_shape=jax.ShapeDtypeStruct((n, d), F32),
        scratch_shapes=[pltpu.VMEM((tm, d), BF16)],
        compiler_params=_cparams(("arbitrary", "arbitrary")),
        name="ple",
    )(h, w_pg, pe, w_pe)


def _rope_tables(pos):
    half = QK_ROPE // 2
    inv = ROPE_THETA ** (-jnp.arange(half, dtype=F32) / half)
    ang = pos[:, None] * inv[None, :]
    cos = jnp.cos(ang)
    sin = jnp.sin(ang)
    return jnp.concatenate([cos, cos], axis=-1), jnp.concatenate([-sin, sin], axis=-1)


def _tile_rows(t, tm):
    reps = max(1, tm // t.shape[0])
    return jnp.tile(t, (reps, 1)) if reps > 1 else t


def kernel(x_prompt, x_sample, p_prompt, p_sample, cache_ckv, cache_kpe, state_wkv, state_shift, page_table, w_in, mu_shift, g_q, g_kv, w_uq, w_uk, w_uv, rw_w0, rw_w2, rw_a0, rw_a2, rw_g2, rw_kk, rw_ka, rw_rk, gn_g, gn_b, w_o, ln1_g, ln1_b, w_ffn_gate, w_ffn_up, w_ffn_down, ln2_g, ln2_b, w_ple, w_ple_gate):
    depth = w_in.shape[0]
    assert depth == 1, "single-layer trunk"
    bp, tp, d = x_prompt.shape
    bs, ts, _ = x_sample.shape
    q_lora = g_q.shape[1]
    kv_lora = g_kv.shape[1]
    page = cache_ckv.shape[2]
    past_len = page_table.shape[1] * page
    rw_cols = mu_shift.shape[1]
    mla_cols = q_lora + kv_lora + QK_ROPE
    off_gate = mla_cols + rw_cols
    alpha = (2.0 * depth) ** 0.25
    assert d // RW_HEAD * RW_HEAD == d and q_lora == IN_BLK and kv_lora == IN_BLK
    assert GATE_LORA + DECAY_LORA + AAA_LORA + QK_ROPE == IN_BLK

    o_r, o_wlo, o_k, o_v = 0, d, d + DECAY_LORA, 2 * d + DECAY_LORA
    o_alo = 3 * d + DECAY_LORA
    o_glo = o_alo + AAA_LORA

    def rw_perm(t):
        return jnp.concatenate([
            t[..., o_r:o_r + d], t[..., o_k:o_k + d], t[..., o_v:o_v + d],
            t[..., o_glo:o_glo + GATE_LORA], t[..., o_wlo:o_wlo + DECAY_LORA],
            t[..., o_alo:o_alo + AAA_LORA]], axis=-1)

    wi = w_in[0]
    w_rw = rw_perm(wi[:, mla_cols:off_gate])
    w_perm = jnp.concatenate([
        w_rw[:, :3 * d], wi[:, off_gate:], wi[:, :q_lora + kv_lora],
        w_rw[:, 3 * d:], wi[:, q_lora + kv_lora:mla_cols]], axis=-1).astype(BF16)
    cq_block = 5 * d // IN_BLK
    ckv_block = cq_block + 1
    misc_block = cq_block + 2
    ones = jnp.ones((1, IN_BLK), F32)
    gains = jnp.concatenate(
        [jnp.ones((1, 5 * d), F32), g_q[0][None], g_kv[0][None], ones], axis=-1)
    zeros_kpe = jnp.zeros((1, QK_ROPE), F32)
    mu_perm = jnp.concatenate([rw_perm(mu_shift[0])[None], zeros_kpe], axis=-1)

    wq = w_uq[0]
    w_uq_perm = jnp.concatenate([
        wq[:, :, :QK_NOPE].reshape(q_lora, MLA_HEADS * QK_NOPE),
        wq[:, :, QK_NOPE:].reshape(q_lora, MLA_HEADS * QK_ROPE)], axis=-1).astype(BF16)
    w_kv = jnp.concatenate([
        w_uk[0].reshape(kv_lora, MLA_HEADS * QK_NOPE),
        w_uv[0].reshape(kv_lora, MLA_HEADS * V_HEAD)], axis=-1).astype(BF16)
    w_uk_t = jnp.transpose(w_uk[0], (1, 2, 0)).astype(BF16)
    w_uv_t = jnp.transpose(w_uv[0], (1, 0, 2)).astype(BF16)

    lora_w = IN_BLK - GATE_LORA
    w2p = jnp.zeros((lora_w, d), F32).at[:DECAY_LORA].set(rw_w2[0]).astype(BF16)
    a2p = jnp.zeros((lora_w, d), F32).at[DECAY_LORA:DECAY_LORA + AAA_LORA].set(rw_a2[0]).astype(BF16)
    g2 = rw_g2[0].astype(BF16)
    vecs = (mu_perm, rw_w0, rw_a0, rw_kk, rw_ka, rw_rk[0].reshape(1, d), gn_g, gn_b)

    w_o_b = w_o[0].astype(BF16)
    wg_b = w_ffn_gate[0].astype(BF16)
    wu_b = w_ffn_up[0].astype(BF16)
    wd_b = w_ffn_down[0].astype(BF16)
    w_pg_b = w_ple_gate[0].astype(BF16)
    w_pe_b = w_ple[0].astype(BF16)

    def trunk(x3, pe3, pos, shift_prev, wkv_prev, chunk, head_group, attend):
        b, t, _ = x3.shape
        n = b * t
        x = x3.reshape(n, d)
        tm = min(1024, n)
        cos64, sin64 = _rope_tables(pos)
        pad = IN_BLK - QK_ROPE
        c_misc = _tile_rows(jnp.pad(cos64, ((0, 0), (pad, 0)), constant_values=1.0), tm)
        s_misc = _tile_rows(jnp.pad(sin64, ((0, 0), (pad, 0))), tm)
        zp = _in_proj(x, w_perm, gains, c_misc, s_misc, d_model=d, tm=tm)
        ckv = zp[:, ckv_block * IN_BLK:(ckv_block + 1) * IN_BLK]
        kpe = zp[:, misc_block * IN_BLK + pad:]
        tmq = min(512, n)
        c_q = _tile_rows(jnp.tile(cos64, (1, MLA_HEADS)), tmq)
        s_q = _tile_rows(jnp.tile(sin64, (1, MLA_HEADS)), tmq)
        q = _q_proj(zp, cq_block, w_uq_perm, c_q, s_q, tm=tmq)
        o_mla = attend(zp, q, ckv, kpe)
        shift_perm = jnp.concatenate(
            [rw_perm(shift_prev), jnp.zeros((b, QK_ROPE), F32)], axis=-1)[:, None, :]
        o_rw, wkv_new = _rwkv(zp, misc_block, shift_perm, wkv_prev, vecs, w2p, a2p, g2,
                              batch=b, seq=t, chunk=chunk, head_group=head_group)
        h1 = _out_proj(zp, o_mla, o_rw, x, w_o_b, ln1_g, ln1_b, alpha=alpha, tm=min(256, n))
        h2 = _ffn(h1, wg_b, wu_b, wd_b, ln2_g, ln2_b, alpha=alpha, tm=min(512, n), tf=512)
        out = _ple(h2, w_pg_b, pe3.reshape(n, -1), w_pe_b, tm=min(512, n), tn=512)
        last = zp.reshape(b, t, -1)[:, -1]
        rw_last = jnp.concatenate([last[:, :3 * d], last[:, misc_block * IN_BLK:misc_block * IN_BLK + pad]], -1)
        shift_new = jnp.concatenate([
            rw_last[:, 0:d], rw_last[:, 3 * d + GATE_LORA:3 * d + GATE_LORA + DECAY_LORA],
            rw_last[:, d:3 * d], rw_last[:, 3 * d + GATE_LORA + DECAY_LORA:],
            rw_last[:, 3 * d:3 * d + GATE_LORA]], axis=-1)
        return (out.reshape(b, t, d), ckv.reshape(b, t, kv_lora), kpe.reshape(b, t, QK_ROPE),
                shift_new, wkv_new)

    def attend_prompt(zp, q, ckv, kpe):
        n = zp.shape[0]
        kv = _mm(zp, ckv_block, kv_lora, w_kv, tm=min(1024, n), tn=1024, out_dtype=BF16, name="kv_proj")
        return _attn_prompt(q, kv, kpe, batch=bp, seq=tp, tq=min(512, tp))

    n_phys = cache_ckv.shape[1]
    cache_c = cache_ckv.reshape(n_phys, page, kv_lora)
    cache_r = jnp.swapaxes(cache_kpe.reshape(n_phys, page, QK_ROPE), 1, 2)

    def attend_sample(zp, q, ckv, kpe):
        n = zp.shape[0]
        nope_cols = MLA_HEADS * QK_NOPE
        q_lat = _head_mm(q[:, :nope_cols], w_uk_t, out_dtype=BF16, name="q_lat")
        q_lat = q_lat.reshape(bs, ts * MLA_HEADS, kv_lora)
        q_pe = q[:, nope_cols:].reshape(bs, ts * MLA_HEADS, QK_ROPE)
        o_lat = _attn_paged(page_table, q_lat, q_pe, ckv.reshape(bs, ts, kv_lora),
                            kpe.reshape(bs, ts, QK_ROPE), cache_c, cache_r,
                            pages=min(16, page_table.shape[1]))
        return _head_mm(o_lat.reshape(n, MLA_HEADS * kv_lora), w_uv_t, out_dtype=F32, name="o_lat")

    pos_p = jnp.arange(tp, dtype=F32)
    pos_s = past_len + jnp.arange(ts, dtype=F32)
    shift0 = jnp.zeros((bp, rw_cols), F32)
    wkv0 = jnp.zeros((bp, d // RW_HEAD, RW_HEAD, RW_HEAD), F32)
    hp, c1, k1, s1, w1 = trunk(x_prompt, p_prompt[0], pos_p, shift0, wkv0, min(PROMPT_CHUNK, tp),
                               RW_GROUP_PROMPT, attend_prompt)
    hs, c2, k2, s2, w2 = trunk(x_sample, p_sample[0], pos_s, state_shift[0], state_wkv[0], ts,
                               RW_GROUP_SAMPLE, attend_sample)
    return (hp, hs, c1[None], k1[None], w1[None], s1[None], c2[None], k2[None], w2[None], s2[None])
```

```python
import functools

import jax
import jax.numpy as jnp
from jax import lax
from jax.experimental import pallas as pl
from jax.experimental.pallas import tpu as pltpu

F32 = jnp.float32
BF16 = jnp.bfloat16

MLA_HEADS = 16
QK_NOPE = 128
QK_ROPE = 64
V_HEAD = 128
RW_HEAD = 64
DECAY_LORA = 96
AAA_LORA = 96
GATE_LORA = 256
ROPE_THETA = 10000.0
SM_SCALE = (QK_NOPE + QK_ROPE) ** -0.5
GN_EPS = 64e-5
LN_EPS = 1e-5
RMS_EPS = 1e-6
DECAY_SCALE = 0.6065306597126334
PROMPT_CHUNK = 64
RW_GROUP_PROMPT = 16
RW_GROUP_SAMPLE = 32

LANES = 128
SEG_BLK = 256
VMEM_LIMIT_BYTES = 56 * 1024 * 1024

NEG_INF = float("-inf")


def _cparams(sem, flags=None):
    return pltpu.CompilerParams(dimension_semantics=sem, vmem_limit_bytes=VMEM_LIMIT_BYTES, flags=flags)


def _dot(a, b):
    return jnp.dot(a.astype(BF16), b.astype(BF16), preferred_element_type=F32)


def _dot_nt(a, b):
    return lax.dot_general(a.astype(BF16), b.astype(BF16), (((1,), (1,)), ((), ())),
                           preferred_element_type=F32)


def _dot_tn(a, b):
    return lax.dot_general(a.astype(BF16), b.astype(BF16), (((0,), (0,)), ((), ())),
                           preferred_element_type=F32)


def _sigmoid(x):
    return 0.5 * jnp.tanh(0.5 * x) + 0.5


def _swap_halves(x, half):
    n = x.shape[-1]
    lane = lax.broadcasted_iota(jnp.int32, x.shape, x.ndim - 1)
    first = (lane % (2 * half)) < half
    return jnp.where(first, pltpu.roll(x, n - half, x.ndim - 1), pltpu.roll(x, half, x.ndim - 1))


IN_BLK = 512


def _in_proj_kernel(x_ref, w_ref, g_ref, c_ref, s_ref, o_ref, xb_ref, *, n_raw, n_sig):
    j = pl.program_id(1)

    @pl.when(j == 0)
    def _():
        xb_ref[...] = x_ref[...].astype(BF16)

    z = jnp.dot(xb_ref[...], w_ref[...], preferred_element_type=F32)

    @pl.when(j < n_raw)
    def _():
        o_ref[...] = z

    @pl.when((j >= n_raw) & (j < n_raw + n_sig))
    def _():
        o_ref[...] = _sigmoid(z)

    @pl.when((j >= n_raw + n_sig) & (j < n_raw + n_sig + 2))
    def _():
        ms = jnp.mean(z * z, axis=-1, keepdims=True)
        o_ref[...] = z * lax.rsqrt(ms + RMS_EPS) * g_ref[...]

    @pl.when(j == n_raw + n_sig + 2)
    def _():
        keep = IN_BLK - LANES
        tail = z[:, keep:]
        o_ref[:, :keep] = z[:, :keep]
        o_ref[:, keep:] = tail * c_ref[...] + _swap_halves(tail, QK_ROPE // 2) * s_ref[...]


def _in_proj(x, w_perm, gains, ctab, stab, *, d_model, tm):
    n = x.shape[0]
    cols = w_perm.shape[1]
    nj = cols // IN_BLK
    n_raw = 3 * d_model // IN_BLK
    n_sig = 2 * d_model // IN_BLK
    assert nj == n_raw + n_sig + 3
    ntab = ctab.shape[0] // tm
    return pl.pallas_call(
        functools.partial(_in_proj_kernel, n_raw=n_raw, n_sig=n_sig),
        grid=(n // tm, nj),
        in_specs=[
            pl.BlockSpec((tm, d_model), lambda i, j: (i, 0)),
            pl.BlockSpec((d_model, IN_BLK), lambda i, j: (0, j)),
            pl.BlockSpec((1, IN_BLK), lambda i, j: (0, j)),
            pl.BlockSpec((tm, LANES), lambda i, j: (i % ntab, 0)),
            pl.BlockSpec((tm, LANES), lambda i, j: (i % ntab, 0)),
        ],
        out_specs=pl.BlockSpec((tm, IN_BLK), lambda i, j: (i, j)),
        out_shape=jax.ShapeDtypeStruct((n, cols), F32),
        scratch_shapes=[pltpu.VMEM((tm, d_model), BF16)],
        compiler_params=_cparams(("arbitrary", "arbitrary")),
        name="in_proj",
    )(x, w_perm, gains, ctab, stab)


def _q_proj_kernel(cq_ref, w_ref, c_ref, s_ref, o_ref, *, n_nope_blocks):
    j = pl.program_id(1)
    z = _dot(cq_ref[...], w_ref[...])

    @pl.when(j < n_nope_blocks)
    def _():
        o_ref[...] = z.astype(o_ref.dtype)

    @pl.when(j >= n_nope_blocks)
    def _():
        reps = z.shape[1] // LANES
        c = jnp.concatenate([c_ref[...]] * reps, axis=1)
        s = jnp.concatenate([s_ref[...]] * reps, axis=1)
        o_ref[...] = (z * c + _swap_halves(z, QK_ROPE // 2) * s).astype(o_ref.dtype)


def _q_proj(zp, cq_block, w_uq_perm, ctab, stab, *, tm):
    n = zp.shape[0]
    q_lora = w_uq_perm.shape[0]
    cols = w_uq_perm.shape[1]
    tn = MLA_HEADS * QK_ROPE
    ntab = ctab.shape[0] // tm
    return pl.pallas_call(
        functools.partial(_q_proj_kernel, n_nope_blocks=MLA_HEADS * QK_NOPE // tn),
        grid=(n // tm, cols // tn),
        in_specs=[
            pl.BlockSpec((tm, q_lora), lambda i, j: (i, cq_block)),
            pl.BlockSpec((q_lora, tn), lambda i, j: (0, j)),
            pl.BlockSpec((tm, LANES), lambda i, j: (i % ntab, 0)),
            pl.BlockSpec((tm, LANES), lambda i, j: (i % ntab, 0)),
        ],
        out_specs=pl.BlockSpec((tm, tn), lambda i, j: (i, j)),
        out_shape=jax.ShapeDtypeStruct((n, cols), BF16),
        compiler_params=_cparams(("arbitrary", "arbitrary")),
        name="q_proj",
    )(zp, w_uq_perm, ctab, stab)


def _mm_kernel(x_ref, w_ref, o_ref):
    o_ref[...] = _dot(x_ref[...], w_ref[...]).astype(o_ref.dtype)


def _mm(x, x_block, k, w, *, tm, tn, out_dtype, name):
    n = x.shape[0]
    cols = w.shape[1]
    return pl.pallas_call(
        _mm_kernel,
        grid=(n // tm, cols // tn),
        in_specs=[
            pl.BlockSpec((tm, k), lambda i, j: (i, x_block)),
            pl.BlockSpec((k, tn), lambda i, j: (0, j)),
        ],
        out_specs=pl.BlockSpec((tm, tn), lambda i, j: (i, j)),
        out_shape=jax.ShapeDtypeStruct((n, cols), out_dtype),
        compiler_params=_cparams(("arbitrary", "arbitrary")),
        name=name,
    )(x, w)


def _head_mm_kernel(x_ref, w_ref, o_ref):
    o_ref[...] = _dot(x_ref[...], w_ref[0]).astype(o_ref.dtype)


def _head_mm(x, w, *, out_dtype, name):
    n = x.shape[0]
    h, k, m = w.shape
    return pl.pallas_call(
        _head_mm_kernel,
        grid=(h,),
        in_specs=[
            pl.BlockSpec((n, k), lambda i: (0, i)),
            pl.BlockSpec((1, k, m), lambda i: (i, 0, 0)),
        ],
        out_specs=pl.BlockSpec((n, m), lambda i: (0, i)),
        out_shape=jax.ShapeDtypeStruct((n, h * m), out_dtype),
        compiler_params=_cparams(("arbitrary",)),
        name=name,
    )(x, w)


HEADS_PER_STEP = 8
ROW_SPLIT = 2


def _softmax_update(s, m_prev, l_prev):
    m_next = jnp.maximum(m_prev, jnp.max(s, axis=-1, keepdims=True))
    alpha = jnp.exp(m_prev - m_next)
    p = jnp.exp(s - m_next)
    l_next = alpha * l_prev + jnp.sum(p, axis=-1, keepdims=True)
    return m_next, l_next, alpha, p.astype(BF16)


def _attn_prompt_kernel(qi_ref, ki_ref, qn_ref, qp_ref, kn_ref, v_ref, kpe_ref, o_ref,
                        m_ref, l_ref, acc_ref, *, tq, tk):
    step = pl.program_id(2)
    qi = qi_ref[step]
    ki = ki_ref[step]
    heads = range(HEADS_PER_STEP)

    @pl.when(ki == 0)
    def _():
        m_ref[...] = jnp.full(m_ref.shape, NEG_INF, F32)
        l_ref[...] = jnp.zeros(l_ref.shape, F32)
        acc_ref[...] = jnp.zeros(acc_ref.shape, F32)

    def block(diagonal):
        kpe = kpe_ref[...].astype(BF16)
        rq = tq // ROW_SPLIT
        units = [(h, r * rq) for h in heads for r in range(ROW_SPLIT)]
        s = [(_dot_nt(qn_ref[r0:r0 + rq, h * QK_NOPE:(h + 1) * QK_NOPE],
                      kn_ref[:, h * QK_NOPE:(h + 1) * QK_NOPE])
              + _dot_nt(qp_ref[r0:r0 + rq, h * QK_ROPE:(h + 1) * QK_ROPE], kpe)) * SM_SCALE
             for h, r0 in units]
        if diagonal:
            s = [jnp.where(lax.broadcasted_iota(jnp.int32, (rq, tk), 1)
                           <= lax.broadcasted_iota(jnp.int32, (rq, tk), 0) + r0, su, NEG_INF)
                 for su, (h, r0) in zip(s, units)]
        upd = [_softmax_update(su, m_ref[h, r0:r0 + rq], l_ref[h, r0:r0 + rq])
               for su, (h, r0) in zip(s, units)]
        for (m_next, l_next, alpha, p), (h, r0) in zip(upd, units):
            m_ref[h, r0:r0 + rq] = m_next
            l_ref[h, r0:r0 + rq] = l_next
            acc_ref[h, r0:r0 + rq] = (alpha * acc_ref[h, r0:r0 + rq]
                                      + _dot(p, v_ref[:, h * V_HEAD:(h + 1) * V_HEAD]))

    @pl.when(ki < qi)
    def _():
        block(False)

    @pl.when(ki == qi)
    def _():
        block(True)
        for h in heads:
            o_ref[:, h * V_HEAD:(h + 1) * V_HEAD] = acc_ref[h] / l_ref[h]


def _attn_prompt(q, kv, kpe, *, batch, seq, tq):
    tk = tq
    nq = seq // tq
    qi_tab = jnp.asarray([qi for qi in range(nq) for _ in range(qi + 1)], jnp.int32)
    ki_tab = jnp.asarray([ki for qi in range(nq) for ki in range(qi + 1)], jnp.int32)
    n_pairs = MLA_HEADS // HEADS_PER_STEP
    nope_w = HEADS_PER_STEP * QK_NOPE
    pe_w = HEADS_PER_STEP * QK_ROPE
    v_w = HEADS_PER_STEP * V_HEAD
    pe_off = MLA_HEADS * QK_NOPE // pe_w
    v_off = MLA_HEADS * QK_NOPE // v_w
    grid_spec = pltpu.PrefetchScalarGridSpec(
        num_scalar_prefetch=2,
        grid=(batch, n_pairs, int(qi_tab.shape[0])),
        in_specs=[
            pl.BlockSpec((tq, nope_w), lambda b, hp, s, qt, kt: (b * nq + qt[s], hp)),
            pl.BlockSpec((tq, pe_w), lambda b, hp, s, qt, kt: (b * nq + qt[s], pe_off + hp)),
            pl.BlockSpec((tk, nope_w), lambda b, hp, s, qt, kt: (b * nq + kt[s], hp)),
            pl.BlockSpec((tk, v_w), lambda b, hp, s, qt, kt: (b * nq + kt[s], v_off + hp)),
            pl.BlockSpec((tk, QK_ROPE), lambda b, hp, s, qt, kt: (b * nq + kt[s], 0)),
        ],
        out_specs=pl.BlockSpec((tq, v_w), lambda b, hp, s, qt, kt: (b * nq + qt[s], hp)),
        scratch_shapes=[
            pltpu.VMEM((HEADS_PER_STEP, tq, 1), F32),
            pltpu.VMEM((HEADS_PER_STEP, tq, 1), F32),
            pltpu.VMEM((HEADS_PER_STEP, tq, V_HEAD), F32),
        ],
    )
    return pl.pallas_call(
        functools.partial(_attn_prompt_kernel, tq=tq, tk=tk),
        grid_spec=grid_spec,
        out_shape=jax.ShapeDtypeStruct((batch * seq, MLA_HEADS * V_HEAD), F32),
        compiler_params=_cparams(("arbitrary", "arbitrary", "arbitrary")),
        name="attn_prompt",
    )(qi_tab, ki_tab, q, q, kv, kv, kpe)


PAGE_SLOTS = 3


def _attn_paged_kernel(pt_ref, ql_ref, qp_ref, cn_ref, kn_ref, ckv_hbm, kpet_hbm, o_ref,
                       kbuf, pbuf, sem, m_ref, l_ref, acc_ref, *, pages, page, t_new):
    b = pl.program_id(0)
    g = pl.program_id(1)
    nb = pl.num_programs(0)
    ng = pl.num_programs(1)
    lin = b * ng + g

    groups = [(b, g, lax.rem(lin, PAGE_SLOTS))]
    for _ in range(PAGE_SLOTS - 1):
        pb, pg, ps = groups[-1]
        wrap = pg + 1 == ng
        groups.append((jnp.where(wrap, jnp.where(pb + 1 == nb, 0, pb + 1), pb),
                       jnp.where(wrap, 0, pg + 1),
                       jnp.where(ps + 1 == PAGE_SLOTS, 0, ps + 1)))

    def copies(ahead):
        bb, gg, sl = groups[ahead]
        out = []
        for i in range(pages):
            pid = pt_ref[bb, gg * pages + i]
            out.append(pltpu.make_async_copy(ckv_hbm.at[pid], kbuf.at[sl, i], sem.at[0, sl]))
            out.append(pltpu.make_async_copy(kpet_hbm.at[pid], pbuf.at[sl, i], sem.at[1, sl]))
        return out

    @pl.when(lin == 0)
    def _():
        for n in range(PAGE_SLOTS - 1):
            for k, c in enumerate(copies(n)):
                c.start(priority=(k // 2) % 2)

    @pl.when(g == 0)
    def _():
        m_ref[...] = jnp.full(m_ref.shape, NEG_INF, F32)
        l_ref[...] = jnp.zeros(l_ref.shape, F32)
        acc_ref[...] = jnp.zeros(acc_ref.shape, F32)

    for c in copies(0):
        c.wait()

    slot = groups[0][2]
    ahead = copies(PAGE_SLOTS - 1)
    ql = ql_ref[0]
    qp = qp_ref[0]
    keys, s = [], []
    for i in range(pages):
        ahead[2 * i].start(priority=i % 2)
        ahead[2 * i + 1].start(priority=i % 2)
        keys.append(kbuf[slot, i].astype(BF16))
        s.append(_dot_nt(ql, keys[i]) + _dot(qp, pbuf[slot, i]))
    half = pages // 2
    m_run, l_run, acc = m_ref[...], l_ref[...], acc_ref[...]
    for lo, hi in ((0, half), (half, pages)):
        s_part = jnp.concatenate(s[lo:hi], axis=-1) * SM_SCALE
        m_run, l_run, alpha, p = _softmax_update(s_part, m_run, l_run)
        pv = _dot(p[:, 0:page], keys[lo])
        for i in range(lo + 1, hi):
            pv = pv + _dot(p[:, (i - lo) * page:(i - lo + 1) * page], keys[i])
        acc = alpha * acc + pv
    m_ref[...] = m_run
    l_ref[...] = l_run
    acc_ref[...] = acc

    @pl.when(lin == nb * ng - 1)
    def _():
        for n in range(1, PAGE_SLOTS):
            for c in copies(n):
                c.wait()

    @pl.when(g == ng - 1)
    def _():
        cn = cn_ref[0].astype(BF16)
        rows = ql.shape[0]
        sn = (_dot_nt(ql, cn) + _dot_nt(qp, kn_ref[0])) * SM_SCALE
        t_row = lax.broadcasted_iota(jnp.int32, (rows, t_new), 0) // (rows // t_new)
        t_col = lax.broadcasted_iota(jnp.int32, (rows, t_new), 1)
        sn = jnp.where(t_col <= t_row, sn, NEG_INF)
        m_prev = m_ref[...]
        m_next = jnp.maximum(m_prev, jnp.max(sn, axis=-1, keepdims=True))
        alpha = jnp.exp(m_prev - m_next)
        p = jnp.exp(sn - m_next)
        l = alpha * l_ref[...] + jnp.sum(p, axis=-1, keepdims=True)
        acc = alpha * acc_ref[...] + _dot(p, cn)
        o_ref[0] = (acc / l).astype(o_ref.dtype)


def _attn_paged(page_table, q_lat, q_pe, ckv_new, kpe_new, cache_ckv, cache_kpe_t, *, pages):
    nseq, rows, kv_lora = q_lat.shape
    t_new = ckv_new.shape[1]
    page = cache_ckv.shape[1]
    n_pages = page_table.shape[1]
    assert n_pages % pages == 0 and pages % 2 == 0
    in_specs = [
        pl.BlockSpec((1, rows, kv_lora), lambda b, g, pt: (b, 0, 0)),
        pl.BlockSpec((1, rows, QK_ROPE), lambda b, g, pt: (b, 0, 0)),
        pl.BlockSpec((1, t_new, kv_lora), lambda b, g, pt: (b, 0, 0)),
        pl.BlockSpec((1, t_new, QK_ROPE), lambda b, g, pt: (b, 0, 0)),
        pl.BlockSpec(memory_space=pl.ANY),
        pl.BlockSpec(memory_space=pl.ANY),
    ]
    grid_spec = pltpu.PrefetchScalarGridSpec(
        num_scalar_prefetch=1,
        grid=(nseq, n_pages // pages),
        in_specs=in_specs,
        out_specs=pl.BlockSpec((1, rows, kv_lora), lambda b, g, pt: (b, 0, 0)),
        scratch_shapes=[
            pltpu.VMEM((PAGE_SLOTS, pages, page, kv_lora), F32),
            pltpu.VMEM((PAGE_SLOTS, pages, QK_ROPE, page), F32),
            pltpu.SemaphoreType.DMA((2, PAGE_SLOTS)),
            pltpu.VMEM((rows, 1), F32),
            pltpu.VMEM((rows, 1), F32),
            pltpu.VMEM((rows, kv_lora), F32),
        ],
    )
    return pl.pallas_call(
        functools.partial(_attn_paged_kernel, pages=pages, page=page, t_new=t_new),
        grid_spec=grid_spec,
        out_shape=jax.ShapeDtypeStruct((nseq, rows, kv_lora), BF16),
        compiler_params=_cparams(("arbitrary", "arbitrary")),
        name="attn_paged",
    )(page_table, q_lat, q_pe, ckv_new, kpe_new, cache_ckv, cache_kpe_t)


def _rwkv_kernel(r_ref, k_ref, v_ref, misc_ref, shift_ref, s0_ref,
                 mu_ref, w0_ref, a0_ref, kkw_ref, kaw_ref, rk_ref, gng_ref, gnb_ref,
                 w2_ref, a2_ref, g2_ref,
                 o_ref, sout_ref,
                 prev_ref, st_ref, wl_ref, rt_ref, kt_ref, vv_ref, at_ref, bt_ref, bon_ref, gate_ref,
                 y_ref, *, chunk, heads, head_group):
    c = pl.program_id(1)
    nc = pl.num_programs(1)
    d = r_ref.shape[1]
    misc_w = misc_ref.shape[1]
    lora_w = w2_ref.shape[0]

    @pl.when(c == 0)
    def _():
        prev_ref[...] = shift_ref[0]
        st_ref[...] = s0_ref[0]

    row0 = lax.broadcasted_iota(jnp.int32, (chunk, 1), 0) == 0

    def mix(z, lo, width):
        prev = jnp.where(row0, prev_ref[:, lo:lo + width], pltpu.roll(z, 1, 0))
        return z + (prev - z) * mu_ref[:, lo:lo + width]

    r_raw = r_ref[...]
    k_raw = k_ref[...]
    v_raw = v_ref[...]
    m_raw = misc_ref[...]
    r = mix(r_raw, 0, d)
    k = mix(k_raw, d, d)
    v = mix(v_raw, 2 * d, d)
    m = mix(m_raw, 3 * d, misc_w)
    last = chunk - 1
    prev_ref[:, 0:d] = r_raw[last:last + 1]
    prev_ref[:, d:2 * d] = k_raw[last:last + 1]
    prev_ref[:, 2 * d:3 * d] = v_raw[last:last + 1]
    prev_ref[:, 3 * d:3 * d + misc_w] = m_raw[last:last + 1]

    g_lo = m[:, 0:GATE_LORA]
    lo = m[:, GATE_LORA:GATE_LORA + lora_w]
    w_pre = w0_ref[...] + _dot(jnp.tanh(lo), w2_ref[...])
    a = _sigmoid(a0_ref[...] + _dot(lo, a2_ref[...]))
    gate = _dot(_sigmoid(g_lo), g2_ref[...])
    lw = -DECAY_SCALE * _sigmoid(w_pre)
    ti = lax.broadcasted_iota(jnp.int32, (chunk, chunk), 0)
    tj = lax.broadcasted_iota(jnp.int32, (chunk, chunk), 1)
    incl = ti >= tj
    strict = ti > tj
    tri = incl.astype(BF16)
    hi = lw.astype(BF16)
    mid = (lw - hi.astype(F32)).astype(BF16)
    low = (lw - hi.astype(F32) - mid.astype(F32)).astype(BF16)
    cs = (jnp.dot(tri, hi, preferred_element_type=F32) + jnp.dot(tri, mid, preferred_element_type=F32)
          + jnp.dot(tri, low, preferred_element_type=F32))
    gi_ = lax.broadcasted_iota(jnp.int32, (SEG_BLK, SEG_BLK), 0) // RW_HEAD
    gj_ = lax.broadcasted_iota(jnp.int32, (SEG_BLK, SEG_BLK), 1) // RW_HEAD
    seg = (gi_ == gj_).astype(BF16)

    def head_sum(x):
        x_hi = x.astype(BF16).astype(F32)
        x_lo = x - x_hi
        cols = range(0, d, SEG_BLK)
        stacked = jnp.concatenate([x_hi[:, j:j + SEG_BLK] for j in cols]
                                  + [x_lo[:, j:j + SEG_BLK] for j in cols], axis=0)
        sums = jnp.dot(stacked.astype(BF16), seg, preferred_element_type=F32)
        nblk = len(cols)
        return jnp.concatenate(
            [sums[j * chunk:(j + 1) * chunk] + sums[(nblk + j) * chunk:(nblk + j + 1) * chunk]
             for j in range(nblk)], axis=1)

    w_inv = jnp.exp(-cs)
    kp = k * (1.0 + (a - 1.0) * kaw_ref[...])
    kkr = k * kkw_ref[...]
    kk = kkr * lax.rsqrt(jnp.maximum(head_sum(kkr * kkr), 1e-24))
    w_c = jnp.exp(cs)
    rt_ref[...] = r * w_c
    kt_ref[...] = kp * w_inv
    vv_ref[...] = v
    at_ref[...] = -kk * jnp.exp(cs - lw)
    bt_ref[...] = kk * a * w_inv
    wl_ref[...] = w_c[last:last + 1]
    bon_ref[...] = head_sum(r * kp * rk_ref[...])
    gate_ref[...] = gate

    n_double = max(1, (chunk - 1).bit_length())
    gw = head_group * RW_HEAD

    def group_body(gi, carry):
        off = pl.multiple_of(gi * gw, gw)
        sl = pl.ds(off, gw)
        rt_g = rt_ref[:, sl]
        kt_g = kt_ref[:, sl]
        v_g = vv_ref[:, sl]
        at_g = at_ref[:, sl]
        bt_g = bt_ref[:, sl]
        wl_g = wl_ref[:, sl]
        s0s = [st_ref[gi * head_group + i] for i in range(head_group)]
        hrange = range(head_group)
        hsl = [slice(i * RW_HEAD, (i + 1) * RW_HEAD) for i in hrange]
        vh = [v_g[:, hs] for hs in hsl]
        lhs = [jnp.concatenate([at_g[:, hs], rt_g[:, hs]], axis=0) for hs in hsl]
        rhs = [jnp.concatenate([bt_g[:, hs], kt_g[:, hs]], axis=0) for hs in hsl]
        gram = [_dot_nt(lhs[i], rhs[i]) for i in hrange]
        h0 = [_dot_nt(lhs[i], s0s[i]) for i in hrange]
        pw = [jnp.where(strict, gram[i][:chunk, :chunk], 0.0) for i in hrange]
        u = [h0[i][:chunk] + _dot(jnp.where(strict, gram[i][:chunk, chunk:], 0.0), vh[i]) for i in hrange]
        for step in range(n_double):
            u = [u[i] + _dot(pw[i], u[i]) for i in hrange]
            if step + 1 < n_double:
                pw = [_dot(pw[i], pw[i]) for i in hrange]
        uv = [jnp.concatenate([u[i], vh[i]], axis=0) for i in hrange]
        ys = []
        for i in hrange:
            a_r = jnp.concatenate([jnp.where(incl, gram[i][chunk:, :chunk], 0.0),
                                   jnp.where(incl, gram[i][chunk:, chunk:], 0.0)], axis=1)
            ys.append(h0[i][chunk:] + _dot(a_r, uv[i]))
        s_new = [(s0s[i] + _dot_tn(uv[i], rhs[i])) * wl_g[:, hsl[i]] for i in hrange]
        y_ref[:, sl] = jnp.concatenate(ys, axis=-1)
        for i in range(head_group):
            st_ref[gi * head_group + i] = s_new[i]
        return carry

    lax.fori_loop(0, heads // head_group, group_body, 0)
    y = y_ref[...]
    yc = y - head_sum(y) * (1.0 / RW_HEAD)
    var = head_sum(yc * yc) * (1.0 / RW_HEAD)
    yn = yc * lax.rsqrt(var + GN_EPS) * gng_ref[...] + gnb_ref[...]
    o_ref[...] = (yn + bon_ref[...] * vv_ref[...]) * gate_ref[...]

    @pl.when(c == nc - 1)
    def _():
        sout_ref[0] = st_ref[...]


def _rwkv(zp, misc_block, shift_perm, s0, vecs, w2p, a2p, g2, *, batch, seq, chunk, head_group):
    d = g2.shape[1]
    heads = d // RW_HEAD
    nc = seq // chunk
    misc_w = IN_BLK
    row = lambda b, c: b * nc + c
    full = lambda shape: pl.BlockSpec(shape, lambda b, c: (0,) * len(shape))
    mu, w0, a0, kkw, kaw, rk, gng, gnb = vecs
    big = pltpu.VMEM((chunk, d), F32)
    return pl.pallas_call(
        functools.partial(_rwkv_kernel, chunk=chunk, heads=heads, head_group=head_group),
        grid=(batch, nc),
        in_specs=[
            pl.BlockSpec((chunk, d), lambda b, c: (row(b, c), 0)),
            pl.BlockSpec((chunk, d), lambda b, c: (row(b, c), 1)),
            pl.BlockSpec((chunk, d), lambda b, c: (row(b, c), 2)),
            pl.BlockSpec((chunk, misc_w), lambda b, c: (row(b, c), misc_block)),
            pl.BlockSpec((1, 1, 3 * d + misc_w), lambda b, c: (b, 0, 0)),
            pl.BlockSpec((1, heads, RW_HEAD, RW_HEAD), lambda b, c: (b, 0, 0, 0)),
            full((1, 3 * d + misc_w)),
            full((1, d)), full((1, d)), full((1, d)), full((1, d)), full((1, d)), full((1, d)), full((1, d)),
            full(w2p.shape), full(a2p.shape), full(g2.shape),
        ],
        out_specs=[
            pl.BlockSpec((chunk, d), lambda b, c: (row(b, c), 0)),
            pl.BlockSpec((1, heads, RW_HEAD, RW_HEAD), lambda b, c: (b, 0, 0, 0)),
        ],
        out_shape=[
            jax.ShapeDtypeStruct((batch * seq, d), F32),
            jax.ShapeDtypeStruct((batch, heads, RW_HEAD, RW_HEAD), F32),
        ],
        scratch_shapes=[
            pltpu.VMEM((1, 3 * d + misc_w), F32),
            pltpu.VMEM((heads, RW_HEAD, RW_HEAD), F32),
            pltpu.VMEM((1, d), F32),
            big, big, big, big, big, big, big, big,
        ],
        compiler_params=_cparams(("arbitrary", "arbitrary")),
        name="rwkv",
    )(zp, zp, zp, zp, shift_perm, s0, mu, w0, a0, kkw, kaw, rk, gng, gnb, w2p, a2p, g2)


def _layer_norm(x, g, b):
    mu = jnp.mean(x, axis=-1, keepdims=True)
    xc = x - mu
    var = jnp.mean(xc * xc, axis=-1, keepdims=True)
    return xc * lax.rsqrt(var + LN_EPS) * g + b


def _out_proj_kernel(gm_ref, gr_ref, om_ref, or_ref, x_ref, w_ref, g_ref, b_ref, o_ref, *, alpha):
    merged = gm_ref[...] * om_ref[...] + gr_ref[...] * or_ref[...]
    y = alpha * x_ref[...] + _dot(merged, w_ref[...])
    o_ref[...] = _layer_norm(y, g_ref[...], b_ref[...])


def _out_proj(zp, o_mla, o_rw, x, w_o, ln_g, ln_b, *, alpha, tm):
    n, d = x.shape
    rows = lambda blk: pl.BlockSpec((tm, d), lambda i: (i, blk))
    const = lambda shape: pl.BlockSpec(shape, lambda i: (0, 0))
    return pl.pallas_call(
        functools.partial(_out_proj_kernel, alpha=alpha),
        grid=(n // tm,),
        in_specs=[rows(3), rows(4), rows(0), rows(0), rows(0), const((d, d)), const((1, d)), const((1, d))],
        out_specs=rows(0),
        out_shape=jax.ShapeDtypeStruct((n, d), F32),
        compiler_params=_cparams(("arbitrary",)),
        name="out_proj",
    )(zp, zp, o_mla, o_rw, x, w_o, ln_g, ln_b)


def _ffn_kernel(h_ref, wg_ref, wu_ref, wd_ref, g_ref, b_ref, o_ref, hb_ref, acc_ref, *, alpha):
    j = pl.program_id(1)

    @pl.when(j == 0)
    def _():
        hb_ref[...] = h_ref[...].astype(BF16)
        acc_ref[...] = jnp.zeros(acc_ref.shape, F32)

    hb = hb_ref[...]
    gate = jnp.dot(hb, wg_ref[...], preferred_element_type=F32)
    up = jnp.dot(hb, wu_ref[...], preferred_element_type=F32)
    acc_ref[...] += _dot(gate * _sigmoid(gate) * up, wd_ref[...])

    @pl.when(j == pl.num_programs(1) - 1)
    def _():
        o_ref[...] = _layer_norm(alpha * h_ref[...] + acc_ref[...], g_ref[...], b_ref[...])


def _ffn(h, wg, wu, wd, ln_g, ln_b, *, alpha, tm, tf):
    n, d = h.shape
    d_ff = wg.shape[1]
    return pl.pallas_call(
        functools.partial(_ffn_kernel, alpha=alpha),
        grid=(n // tm, d_ff // tf),
        in_specs=[
            pl.BlockSpec((tm, d), lambda i, j: (i, 0)),
            pl.BlockSpec((d, tf), lambda i, j: (0, j)),
            pl.BlockSpec((d, tf), lambda i, j: (0, j)),
            pl.BlockSpec((tf, d), lambda i, j: (j, 0)),
            pl.BlockSpec((1, d), lambda i, j: (0, 0)),
            pl.BlockSpec((1, d), lambda i, j: (0, 0)),
        ],
        out_specs=pl.BlockSpec((tm, d), lambda i, j: (i, 0)),
        out_shape=jax.ShapeDtypeStruct((n, d), F32),
        scratch_shapes=[pltpu.VMEM((tm, d), BF16), pltpu.VMEM((tm, d), F32)],
        compiler_params=_cparams(("arbitrary", "arbitrary")),
        name="ffn",
    )(h, wg, wu, wd, ln_g, ln_b)


def _ple_kernel(h_ref, wpg_ref, pe_ref, wpe_ref, o_ref, hb_ref, *, tn):
    j = pl.program_id(1)

    @pl.when(j == 0)
    def _():
        hb_ref[...] = h_ref[...].astype(BF16)

    gate = _sigmoid(jnp.dot(hb_ref[...], wpg_ref[...], preferred_element_type=F32))
    emb = _dot(pe_ref[...], wpe_ref[...])
    o_ref[...] = h_ref[:, pl.ds(pl.multiple_of(j * tn, tn), tn)] + gate * emb


def _ple(h, w_pg, pe, w_pe, *, tm, tn):
    n, d = h.shape
    p = pe.shape[1]
    return pl.pallas_call(
        functools.partial(_ple_kernel, tn=tn),
        grid=(n // tm, d // tn),
        in_specs=[
            pl.BlockSpec((tm, d), lambda i, j: (i, 0)),
            pl.BlockSpec((d, tn), lambda i, j: (0, j)),
            pl.BlockSpec((tm, p), lambda i, j: (i, 0)),
            pl.BlockSpec((p, tn), lambda i, j: (0, j)),
        ],
        out_specs=pl.BlockSpec((tm, tn), lambda i, j: (i, j)),
        out_shape=jax.ShapeDtypeStruct((n, d), F32),
        scratch_shapes=[pltpu.VMEM((tm, d), BF16)],
        compiler_params=_cparams(("arbitrary", "arbitrary")),
        name="ple",
    )(h, w_pg, pe, w_pe)


def _rope_tables(pos):
    half = QK_ROPE // 2
    inv = ROPE_THETA ** (-jnp.arange(half, dtype=F32) / half)
    ang = pos[:, None] * inv[None, :]
    cos = jnp.cos(ang)
    sin = jnp.sin(ang)
    return jnp.concatenate([cos, cos], axis=-1), jnp.concatenate([-sin, sin], axis=-1)


def _tile_rows(t, tm):
    reps = max(1, tm // t.shape[0])
    return jnp.tile(t, (reps, 1)) if reps > 1 else t


def kernel(x_prompt, x_sample, p_prompt, p_sample, cache_ckv, cache_kpe, state_wkv, state_shift, page_table, w_in, mu_shift, g_q, g_kv, w_uq, w_uk, w_uv, rw_w0, rw_w2, rw_a0, rw_a2, rw_g2, rw_kk, rw_ka, rw_rk, gn_g, gn_b, w_o, ln1_g, ln1_b, w_ffn_gate, w_ffn_up, w_ffn_down, ln2_g, ln2_b, w_ple, w_ple_gate):
    depth = w_in.shape[0]
    assert depth == 1, "single-layer trunk"
    bp, tp, d = x_prompt.shape
    bs, ts, _ = x_sample.shape
    q_lora = g_q.shape[1]
    kv_lora = g_kv.shape[1]
    page = cache_ckv.shape[2]
    past_len = page_table.shape[1] * page
    rw_cols = mu_shift.shape[1]
    mla_cols = q_lora + kv_lora + QK_ROPE
    off_gate = mla_cols + rw_cols
    alpha = (2.0 * depth) ** 0.25
    assert d // RW_HEAD * RW_HEAD == d and q_lora == IN_BLK and kv_lora == IN_BLK
    assert GATE_LORA + DECAY_LORA + AAA_LORA + QK_ROPE == IN_BLK and 2 * QK_ROPE == LANES

    o_r, o_wlo, o_k, o_v = 0, d, d + DECAY_LORA, 2 * d + DECAY_LORA
    o_alo = 3 * d + DECAY_LORA
    o_glo = o_alo + AAA_LORA

    def rw_perm(t):
        return jnp.concatenate([
            t[..., o_r:o_r + d], t[..., o_k:o_k + d], t[..., o_v:o_v + d],
            t[..., o_glo:o_glo + GATE_LORA], t[..., o_wlo:o_wlo + DECAY_LORA],
            t[..., o_alo:o_alo + AAA_LORA]], axis=-1)

    wi = w_in[0]
    w_rw = rw_perm(wi[:, mla_cols:off_gate])
    w_perm = jnp.concatenate([
        w_rw[:, :3 * d], wi[:, off_gate:], wi[:, :q_lora + kv_lora],
        w_rw[:, 3 * d:], wi[:, q_lora + kv_lora:mla_cols]], axis=-1).astype(BF16)
    cq_block = 5 * d // IN_BLK
    ckv_block = cq_block + 1
    misc_block = cq_block + 2
    ones = jnp.ones((1, IN_BLK), F32)
    gains = jnp.concatenate(
        [jnp.ones((1, 5 * d), F32), g_q[0][None], g_kv[0][None], ones], axis=-1)
    zeros_kpe = jnp.zeros((1, QK_ROPE), F32)
    mu_perm = jnp.concatenate([rw_perm(mu_shift[0])[None], zeros_kpe], axis=-1)

    wq = w_uq[0]
    w_uq_perm = jnp.concatenate([
        wq[:, :, :QK_NOPE].reshape(q_lora, MLA_HEADS * QK_NOPE),
        wq[:, :, QK_NOPE:].reshape(q_lora, MLA_HEADS * QK_ROPE)], axis=-1).astype(BF16)
    w_kv = jnp.concatenate([
        w_uk[0].reshape(kv_lora, MLA_HEADS * QK_NOPE),
        w_uv[0].reshape(kv_lora, MLA_HEADS * V_HEAD)], axis=-1).astype(BF16)
    w_uk_t = jnp.transpose(w_uk[0], (1, 2, 0)).astype(BF16)
    w_uv_t = jnp.transpose(w_uv[0], (1, 0, 2)).astype(BF16)

    lora_w = IN_BLK - GATE_LORA
    w2p = jnp.zeros((lora_w, d), F32).at[:DECAY_LORA].set(rw_w2[0]).astype(BF16)
    a2p = jnp.zeros((lora_w, d), F32).at[DECAY_LORA:DECAY_LORA + AAA_LORA].set(rw_a2[0]).astype(BF16)
    g2 = rw_g2[0].astype(BF16)
    vecs = (mu_perm, rw_w0, rw_a0, rw_kk, rw_ka, rw_rk[0].reshape(1, d), gn_g, gn_b)

    w_o_b = w_o[0].astype(BF16)
    wg_b = w_ffn_gate[0].astype(BF16)
    wu_b = w_ffn_up[0].astype(BF16)
    wd_b = w_ffn_down[0].astype(BF16)
    w_pg_b = w_ple_gate[0].astype(BF16)
    w_pe_b = w_ple[0].astype(BF16)

    def trunk(x3, pe3, pos, shift_prev, wkv_prev, chunk, head_group, attend):
        b, t, _ = x3.shape
        n = b * t
        x = x3.reshape(n, d)
        tm = min(1024, n)
        cos64, sin64 = _rope_tables(pos)
        pad = IN_BLK - QK_ROPE
        c_misc = _tile_rows(jnp.concatenate([jnp.ones_like(cos64), cos64], axis=1), tm)
        s_misc = _tile_rows(jnp.concatenate([jnp.zeros_like(sin64), sin64], axis=1), tm)
        zp = _in_proj(x, w_perm, gains, c_misc, s_misc, d_model=d, tm=tm)
        ckv = zp[:, ckv_block * IN_BLK:(ckv_block + 1) * IN_BLK]
        kpe = zp[:, misc_block * IN_BLK + pad:]
        tmq = min(1024, n)
        c_q = _tile_rows(jnp.concatenate([cos64, cos64], axis=1), tmq)
        s_q = _tile_rows(jnp.concatenate([sin64, sin64], axis=1), tmq)
        q = _q_proj(zp, cq_block, w_uq_perm, c_q, s_q, tm=tmq)
        o_mla = attend(zp, q, ckv, kpe)
        shift_perm = jnp.concatenate(
            [rw_perm(shift_prev), jnp.zeros((b, QK_ROPE), F32)], axis=-1)[:, None, :]
        o_rw, wkv_new = _rwkv(zp, misc_block, shift_perm, wkv_prev, vecs, w2p, a2p, g2,
                              batch=b, seq=t, chunk=chunk, head_group=head_group)
        h1 = _out_proj(zp, o_mla, o_rw, x, w_o_b, ln1_g, ln1_b, alpha=alpha, tm=min(256, n))
        h2 = _ffn(h1, wg_b, wu_b, wd_b, ln2_g, ln2_b, alpha=alpha, tm=min(512, n), tf=512)
        out = _ple(h2, w_pg_b, pe3.reshape(n, -1), w_pe_b, tm=min(1024, n), tn=1024)
        last = zp.reshape(b, t, -1)[:, -1]
        rw_last = jnp.concatenate([last[:, :3 * d], last[:, misc_block * IN_BLK:misc_block * IN_BLK + pad]], -1)
        shift_new = jnp.concatenate([
            rw_last[:, 0:d], rw_last[:, 3 * d + GATE_LORA:3 * d + GATE_LORA + DECAY_LORA],
            rw_last[:, d:3 * d], rw_last[:, 3 * d + GATE_LORA + DECAY_LORA:],
            rw_last[:, 3 * d:3 * d + GATE_LORA]], axis=-1)
        return (out.reshape(b, t, d), ckv.reshape(b, t, kv_lora), kpe.reshape(b, t, QK_ROPE),
                shift_new, wkv_new)

    def attend_prompt(zp, q, ckv, kpe):
        n = zp.shape[0]
        kv = _mm(zp, ckv_block, kv_lora, w_kv, tm=min(1024, n), tn=2048, out_dtype=BF16, name="kv_proj")
        return _attn_prompt(q, kv, kpe, batch=bp, seq=tp, tq=min(512, tp))

    n_phys = cache_ckv.shape[1]
    cache_c = cache_ckv.reshape(n_phys, page, kv_lora)
    cache_r = jnp.swapaxes(cache_kpe.reshape(n_phys, page, QK_ROPE), 1, 2)

    def attend_sample(zp, q, ckv, kpe):
        n = zp.shape[0]
        nope_cols = MLA_HEADS * QK_NOPE
        q_lat = _head_mm(q[:, :nope_cols], w_uk_t, out_dtype=BF16, name="q_lat")
        q_lat = q_lat.reshape(bs, ts * MLA_HEADS, kv_lora)
        q_pe = q[:, nope_cols:].reshape(bs, ts * MLA_HEADS, QK_ROPE)
        o_lat = _attn_paged(page_table, q_lat, q_pe, ckv.reshape(bs, ts, kv_lora),
                            kpe.reshape(bs, ts, QK_ROPE), cache_c, cache_r,
                            pages=min(32, page_table.shape[1]))
        return _head_mm(o_lat.reshape(n, MLA_HEADS * kv_lora), w_uv_t, out_dtype=F32, name="o_lat")

    pos_p = jnp.arange(tp, dtype=F32)
    pos_s = past_len + jnp.arange(ts, dtype=F32)
    shift0 = jnp.zeros((bp, rw_cols), F32)
    wkv0 = jnp.zeros((bp, d // RW_HEAD, RW_HEAD, RW_HEAD), F32)
    hp, c1, k1, s1, w1 = trunk(x_prompt, p_prompt[0], pos_p, shift0, wkv0, min(PROMPT_CHUNK, tp),
                               RW_GROUP_PROMPT, attend_prompt)
    hs, c2, k2, s2, w2 = trunk(x_sample, p_sample[0], pos_s, state_shift[0], state_wkv[0], ts,
                               RW_GROUP_SAMPLE, attend_sample)
    return (hp, hs, c1[None], k1[None], w1[None], s1[None], c2[None], k2[None], w2[None], s2[None])
```

```python
import functools

import jax
import jax.numpy as jnp
from jax import lax
from jax.experimental import pallas as pl
from jax.experimental.pallas import tpu as pltpu

F32 = jnp.float32
BF16 = jnp.bfloat16

MLA_HEADS = 16
QK_NOPE = 128
QK_ROPE = 64
V_HEAD = 128
RW_HEAD = 64
DECAY_LORA = 96
AAA_LORA = 96
GATE_LORA = 256
ROPE_THETA = 10000.0
SM_SCALE = (QK_NOPE + QK_ROPE) ** -0.5
GN_EPS = 64e-5
LN_EPS = 1e-5
RMS_EPS = 1e-6
DECAY_SCALE = 0.6065306597126334
PROMPT_CHUNK = 64
RW_GROUP_PROMPT = 16
RW_GROUP_SAMPLE = 32

LANES = 128
SEG_BLK = 256
VMEM_LIMIT_BYTES = 56 * 1024 * 1024

NEG_INF = float("-inf")


def _cparams(sem, flags=None):
    return pltpu.CompilerParams(dimension_semantics=sem, vmem_limit_bytes=VMEM_LIMIT_BYTES, flags=flags)


def _dot(a, b):
    return jnp.dot(a.astype(BF16), b.astype(BF16), preferred_element_type=F32)


def _dot_nt(a, b):
    return lax.dot_general(a.astype(BF16), b.astype(BF16), (((1,), (1,)), ((), ())),
                           preferred_element_type=F32)


def _dot_tn(a, b):
    return lax.dot_general(a.astype(BF16), b.astype(BF16), (((0,), (0,)), ((), ())),
                           preferred_element_type=F32)


def _sigmoid(x):
    return 0.5 * jnp.tanh(0.5 * x) + 0.5


def _swap_halves(x, half):
    n = x.shape[-1]
    lane = lax.broadcasted_iota(jnp.int32, x.shape, x.ndim - 1)
    first = (lane % (2 * half)) < half
    return jnp.where(first, pltpu.roll(x, n - half, x.ndim - 1), pltpu.roll(x, half, x.ndim - 1))


IN_BLK = 512


def _in_proj_kernel(x_ref, w_ref, g_ref, c_ref, s_ref, o_ref, xb_ref, *, n_raw, n_sig):
    j = pl.program_id(1)

    @pl.when(j == 0)
    def _():
        xb_ref[...] = x_ref[...].astype(BF16)

    z = jnp.dot(xb_ref[...], w_ref[...], preferred_element_type=F32)

    @pl.when(j < n_raw)
    def _():
        o_ref[...] = z

    @pl.when((j >= n_raw) & (j < n_raw + n_sig))
    def _():
        o_ref[...] = _sigmoid(z)

    @pl.when((j >= n_raw + n_sig) & (j < n_raw + n_sig + 2))
    def _():
        ms = jnp.mean(z * z, axis=-1, keepdims=True)
        o_ref[...] = z * lax.rsqrt(ms + RMS_EPS) * g_ref[...]

    @pl.when(j == n_raw + n_sig + 2)
    def _():
        keep = IN_BLK - LANES
        tail = z[:, keep:]
        o_ref[:, :keep] = z[:, :keep]
        o_ref[:, keep:] = tail * c_ref[...] + _swap_halves(tail, QK_ROPE // 2) * s_ref[...]


def _in_proj(x, w_perm, gains, ctab, stab, *, d_model, tm):
    n = x.shape[0]
    cols = w_perm.shape[1]
    nj = cols // IN_BLK
    n_raw = 3 * d_model // IN_BLK
    n_sig = 2 * d_model // IN_BLK
    assert nj == n_raw + n_sig + 3
    ntab = ctab.shape[0] // tm
    return pl.pallas_call(
        functools.partial(_in_proj_kernel, n_raw=n_raw, n_sig=n_sig),
        grid=(n // tm, nj),
        in_specs=[
            pl.BlockSpec((tm, d_model), lambda i, j: (i, 0)),
            pl.BlockSpec((d_model, IN_BLK), lambda i, j: (0, j)),
            pl.BlockSpec((1, IN_BLK), lambda i, j: (0, j)),
            pl.BlockSpec((tm, LANES), lambda i, j: (i % ntab, 0)),
            pl.BlockSpec((tm, LANES), lambda i, j: (i % ntab, 0)),
        ],
        out_specs=pl.BlockSpec((tm, IN_BLK), lambda i, j: (i, j)),
        out_shape=jax.ShapeDtypeStruct((n, cols), F32),
        scratch_shapes=[pltpu.VMEM((tm, d_model), BF16)],
        compiler_params=_cparams(("arbitrary", "arbitrary")),
        name="in_proj",
    )(x, w_perm, gains, ctab, stab)


def _q_proj_kernel(cq_ref, w_ref, c_ref, s_ref, o_ref, *, n_nope_blocks):
    j = pl.program_id(1)
    z = _dot(cq_ref[...], w_ref[...])

    @pl.when(j < n_nope_blocks)
    def _():
        o_ref[...] = z.astype(o_ref.dtype)

    @pl.when(j >= n_nope_blocks)
    def _():
        reps = z.shape[1] // LANES
        c = jnp.concatenate([c_ref[...]] * reps, axis=1)
        s = jnp.concatenate([s_ref[...]] * reps, axis=1)
        o_ref[...] = (z * c + _swap_halves(z, QK_ROPE // 2) * s).astype(o_ref.dtype)


def _q_proj(zp, cq_block, w_uq_perm, ctab, stab, *, tm):
    n = zp.shape[0]
    q_lora = w_uq_perm.shape[0]
    cols = w_uq_perm.shape[1]
    tn = MLA_HEADS * QK_ROPE
    ntab = ctab.shape[0] // tm
    return pl.pallas_call(
        functools.partial(_q_proj_kernel, n_nope_blocks=MLA_HEADS * QK_NOPE // tn),
        grid=(n // tm, cols // tn),
        in_specs=[
            pl.BlockSpec((tm, q_lora), lambda i, j: (i, cq_block)),
            pl.BlockSpec((q_lora, tn), lambda i, j: (0, j)),
            pl.BlockSpec((tm, LANES), lambda i, j: (i % ntab, 0)),
            pl.BlockSpec((tm, LANES), lambda i, j: (i % ntab, 0)),
        ],
        out_specs=pl.BlockSpec((tm, tn), lambda i, j: (i, j)),
        out_shape=jax.ShapeDtypeStruct((n, cols), BF16),
        compiler_params=_cparams(("arbitrary", "arbitrary")),
        name="q_proj",
    )(zp, w_uq_perm, ctab, stab)


def _mm_kernel(x_ref, w_ref, o_ref):
    o_ref[...] = _dot(x_ref[...], w_ref[...]).astype(o_ref.dtype)


def _mm(x, x_block, k, w, *, tm, tn, out_dtype, name):
    n = x.shape[0]
    cols = w.shape[1]
    return pl.pallas_call(
        _mm_kernel,
        grid=(n // tm, cols // tn),
        in_specs=[
            pl.BlockSpec((tm, k), lambda i, j: (i, x_block)),
            pl.BlockSpec((k, tn), lambda i, j: (0, j)),
        ],
        out_specs=pl.BlockSpec((tm, tn), lambda i, j: (i, j)),
        out_shape=jax.ShapeDtypeStruct((n, cols), out_dtype),
        compiler_params=_cparams(("arbitrary", "arbitrary")),
        name=name,
    )(x, w)


def _head_mm_kernel(x_ref, w_ref, o_ref, *, contract_w_cols):
    dot = _dot_nt if contract_w_cols else _dot
    x = x_ref[...]
    if x.ndim == 3:
        x = x.astype(F32).reshape(x.shape[0] * x.shape[1], x.shape[2])
    res = dot(x, w_ref[...])
    o_ref[...] = res.reshape(o_ref.shape).astype(o_ref.dtype)


def _head_mm(x, w, heads, c, w_block0, *, contract_w_cols, seq_out, out_dtype, name):
    r = w.shape[0]
    m = r if contract_w_cols else c
    if x.ndim == 3:
        nseq, rows, k = x.shape
        t = rows // heads
        n = nseq * t
        x_spec = pl.BlockSpec((nseq, t, k), lambda i: (0, i, 0))
    else:
        n = x.shape[0]
        x_spec = pl.BlockSpec((n, x.shape[1] // heads), lambda i: (0, i))
    if seq_out is None:
        out_spec = pl.BlockSpec((n, m), lambda i: (0, i))
        out_shape = jax.ShapeDtypeStruct((n, heads * m), out_dtype)
    else:
        nseq, t = seq_out
        out_spec = pl.BlockSpec((nseq, t, m), lambda i: (0, i, 0))
        out_shape = jax.ShapeDtypeStruct((nseq, heads * t, m), out_dtype)
    return pl.pallas_call(
        functools.partial(_head_mm_kernel, contract_w_cols=contract_w_cols),
        grid=(heads,),
        in_specs=[x_spec, pl.BlockSpec((r, c), lambda i: (0, w_block0 + i))],
        out_specs=out_spec,
        out_shape=out_shape,
        compiler_params=_cparams(("arbitrary",)),
        name=name,
    )(x, w)


HEADS_PER_STEP = 8
ROW_SPLIT = 2


def _softmax_update(s, m_prev, l_prev):
    m_next = jnp.maximum(m_prev, jnp.max(s, axis=-1, keepdims=True))
    alpha = jnp.exp(m_prev - m_next)
    p = jnp.exp(s - m_next)
    l_next = alpha * l_prev + jnp.sum(p, axis=-1, keepdims=True)
    return m_next, l_next, alpha, p.astype(BF16)


def _attn_prompt_kernel(qi_ref, ki_ref, qn_ref, qp_ref, kn_ref, v_ref, kpe_ref, o_ref,
                        m_ref, l_ref, acc_ref, *, tq, tk):
    step = pl.program_id(2)
    qi = qi_ref[step]
    ki = ki_ref[step]
    heads = range(HEADS_PER_STEP)

    @pl.when(ki == 0)
    def _():
        m_ref[...] = jnp.full(m_ref.shape, NEG_INF, F32)
        l_ref[...] = jnp.zeros(l_ref.shape, F32)
        acc_ref[...] = jnp.zeros(acc_ref.shape, F32)

    def block(diagonal):
        kpe = kpe_ref[...].astype(BF16)
        rq = tq // ROW_SPLIT
        units = [(h, r * rq) for h in heads for r in range(ROW_SPLIT)]
        s = [(_dot_nt(qn_ref[r0:r0 + rq, h * QK_NOPE:(h + 1) * QK_NOPE],
                      kn_ref[:, h * QK_NOPE:(h + 1) * QK_NOPE])
              + _dot_nt(qp_ref[r0:r0 + rq, h * QK_ROPE:(h + 1) * QK_ROPE], kpe)) * SM_SCALE
             for h, r0 in units]
        if diagonal:
            s = [jnp.where(lax.broadcasted_iota(jnp.int32, (rq, tk), 1)
                           <= lax.broadcasted_iota(jnp.int32, (rq, tk), 0) + r0, su, NEG_INF)
                 for su, (h, r0) in zip(s, units)]
        upd = [_softmax_update(su, m_ref[h, r0:r0 + rq], l_ref[h, r0:r0 + rq])
               for su, (h, r0) in zip(s, units)]
        for (m_next, l_next, alpha, p), (h, r0) in zip(upd, units):
            m_ref[h, r0:r0 + rq] = m_next
            l_ref[h, r0:r0 + rq] = l_next
            acc_ref[h, r0:r0 + rq] = (alpha * acc_ref[h, r0:r0 + rq]
                                      + _dot(p, v_ref[:, h * V_HEAD:(h + 1) * V_HEAD]))

    @pl.when(ki < qi)
    def _():
        block(False)

    @pl.when(ki == qi)
    def _():
        block(True)
        for h in heads:
            o_ref[:, h * V_HEAD:(h + 1) * V_HEAD] = acc_ref[h] / l_ref[h]


def _attn_prompt(q, kv, kpe, *, batch, seq, tq):
    tk = tq
    nq = seq // tq
    qi_tab = jnp.asarray([qi for qi in range(nq) for _ in range(qi + 1)], jnp.int32)
    ki_tab = jnp.asarray([ki for qi in range(nq) for ki in range(qi + 1)], jnp.int32)
    n_pairs = MLA_HEADS // HEADS_PER_STEP
    nope_w = HEADS_PER_STEP * QK_NOPE
    pe_w = HEADS_PER_STEP * QK_ROPE
    v_w = HEADS_PER_STEP * V_HEAD
    pe_off = MLA_HEADS * QK_NOPE // pe_w
    v_off = MLA_HEADS * QK_NOPE // v_w
    grid_spec = pltpu.PrefetchScalarGridSpec(
        num_scalar_prefetch=2,
        grid=(batch, n_pairs, int(qi_tab.shape[0])),
        in_specs=[
            pl.BlockSpec((tq, nope_w), lambda b, hp, s, qt, kt: (b * nq + qt[s], hp)),
            pl.BlockSpec((tq, pe_w), lambda b, hp, s, qt, kt: (b * nq + qt[s], pe_off + hp)),
            pl.BlockSpec((tk, nope_w), lambda b, hp, s, qt, kt: (b * nq + kt[s], hp)),
            pl.BlockSpec((tk, v_w), lambda b, hp, s, qt, kt: (b * nq + kt[s], v_off + hp)),
            pl.BlockSpec((tk, QK_ROPE), lambda b, hp, s, qt, kt: (b * nq + kt[s], 0)),
        ],
        out_specs=pl.BlockSpec((tq, v_w), lambda b, hp, s, qt, kt: (b * nq + qt[s], hp)),
        scratch_shapes=[
            pltpu.VMEM((HEADS_PER_STEP, tq, 1), F32),
            pltpu.VMEM((HEADS_PER_STEP, tq, 1), F32),
            pltpu.VMEM((HEADS_PER_STEP, tq, V_HEAD), F32),
        ],
    )
    return pl.pallas_call(
        functools.partial(_attn_prompt_kernel, tq=tq, tk=tk),
        grid_spec=grid_spec,
        out_shape=jax.ShapeDtypeStruct((batch * seq, MLA_HEADS * V_HEAD), F32),
        compiler_params=_cparams(("arbitrary", "arbitrary", "arbitrary")),
        name="attn_prompt",
    )(qi_tab, ki_tab, q, q, kv, kv, kpe)


PAGE_SLOTS = 3


def _attn_paged_kernel(pt_ref, ql_ref, qp_ref, cn_ref, kn_ref, ckv_hbm, kpet_hbm, o_ref,
                       kbuf, pbuf, sem, m_ref, l_ref, acc_ref, *, pages, page, t_new):
    b = pl.program_id(0)
    g = pl.program_id(1)
    nb = pl.num_programs(0)
    ng = pl.num_programs(1)
    lin = b * ng + g

    groups = [(b, g, lax.rem(lin, PAGE_SLOTS))]
    for _ in range(PAGE_SLOTS - 1):
        pb, pg, ps = groups[-1]
        wrap = pg + 1 == ng
        groups.append((jnp.where(wrap, jnp.where(pb + 1 == nb, 0, pb + 1), pb),
                       jnp.where(wrap, 0, pg + 1),
                       jnp.where(ps + 1 == PAGE_SLOTS, 0, ps + 1)))

    def copies(ahead):
        bb, gg, sl = groups[ahead]
        out = []
        for i in range(pages):
            pid = pt_ref[bb, gg * pages + i]
            out.append(pltpu.make_async_copy(ckv_hbm.at[pid], kbuf.at[sl, i], sem.at[0, sl]))
            out.append(pltpu.make_async_copy(kpet_hbm.at[pid], pbuf.at[sl, i], sem.at[1, sl]))
        return out

    @pl.when(lin == 0)
    def _():
        for n in range(PAGE_SLOTS - 1):
            for k, c in enumerate(copies(n)):
                c.start(priority=(k // 2) % 2)

    @pl.when(g == 0)
    def _():
        m_ref[...] = jnp.full(m_ref.shape, NEG_INF, F32)
        l_ref[...] = jnp.zeros(l_ref.shape, F32)
        acc_ref[...] = jnp.zeros(acc_ref.shape, F32)

    for c in copies(0):
        c.wait()

    slot = groups[0][2]
    ahead = copies(PAGE_SLOTS - 1)
    ql = ql_ref[0]
    qp = qp_ref[0]
    keys, s = [], []
    for i in range(pages):
        ahead[2 * i].start(priority=i % 2)
        ahead[2 * i + 1].start(priority=i % 2)
        keys.append(kbuf[slot, i].astype(BF16))
        s.append(_dot_nt(ql, keys[i]) + _dot(qp, pbuf[slot, i]))
    half = pages // 2
    m_run, l_run, acc = m_ref[...], l_ref[...], acc_ref[...]
    for lo, hi in ((0, half), (half, pages)):
        s_part = jnp.concatenate(s[lo:hi], axis=-1) * SM_SCALE
        m_run, l_run, alpha, p = _softmax_update(s_part, m_run, l_run)
        pv = _dot(p[:, 0:page], keys[lo])
        for i in range(lo + 1, hi):
            pv = pv + _dot(p[:, (i - lo) * page:(i - lo + 1) * page], keys[i])
        acc = alpha * acc + pv
    m_ref[...] = m_run
    l_ref[...] = l_run
    acc_ref[...] = acc

    @pl.when(lin == nb * ng - 1)
    def _():
        for n in range(1, PAGE_SLOTS):
            for c in copies(n):
                c.wait()

    @pl.when(g == ng - 1)
    def _():
        cn = cn_ref[0].astype(BF16)
        rows = ql.shape[0]
        sn = (_dot_nt(ql, cn) + _dot_nt(qp, kn_ref[0])) * SM_SCALE
        t_row = lax.broadcasted_iota(jnp.int32, (rows, t_new), 0) % t_new
        t_col = lax.broadcasted_iota(jnp.int32, (rows, t_new), 1)
        sn = jnp.where(t_col <= t_row, sn, NEG_INF)
        m_prev = m_ref[...]
        m_next = jnp.maximum(m_prev, jnp.max(sn, axis=-1, keepdims=True))
        alpha = jnp.exp(m_prev - m_next)
        p = jnp.exp(sn - m_next)
        l = alpha * l_ref[...] + jnp.sum(p, axis=-1, keepdims=True)
        acc = alpha * acc_ref[...] + _dot(p, cn)
        o_ref[0] = (acc / l).astype(o_ref.dtype)


def _attn_paged(page_table, q_lat, q_pe, ckv_new, kpe_new, cache_ckv, cache_kpe_t, *, pages):
    nseq, rows, kv_lora = q_lat.shape
    t_new = ckv_new.shape[1]
    page = cache_ckv.shape[1]
    n_pages = page_table.shape[1]
    assert n_pages % pages == 0 and pages % 2 == 0
    in_specs = [
        pl.BlockSpec((1, rows, kv_lora), lambda b, g, pt: (b, 0, 0)),
        pl.BlockSpec((1, rows, QK_ROPE), lambda b, g, pt: (b, 0, 0)),
        pl.BlockSpec((1, t_new, kv_lora), lambda b, g, pt: (b, 0, 0)),
        pl.BlockSpec((1, t_new, QK_ROPE), lambda b, g, pt: (b, 0, 0)),
        pl.BlockSpec(memory_space=pl.ANY),
        pl.BlockSpec(memory_space=pl.ANY),
    ]
    grid_spec = pltpu.PrefetchScalarGridSpec(
        num_scalar_prefetch=1,
        grid=(nseq, n_pages // pages),
        in_specs=in_specs,
        out_specs=pl.BlockSpec((1, rows, kv_lora), lambda b, g, pt: (b, 0, 0)),
        scratch_shapes=[
            pltpu.VMEM((PAGE_SLOTS, pages, page, kv_lora), F32),
            pltpu.VMEM((PAGE_SLOTS, pages, QK_ROPE, page), F32),
            pltpu.SemaphoreType.DMA((2, PAGE_SLOTS)),
            pltpu.VMEM((rows, 1), F32),
            pltpu.VMEM((rows, 1), F32),
            pltpu.VMEM((rows, kv_lora), F32),
        ],
    )
    return pl.pallas_call(
        functools.partial(_attn_paged_kernel, pages=pages, page=page, t_new=t_new),
        grid_spec=grid_spec,
        out_shape=jax.ShapeDtypeStruct((nseq, rows, kv_lora), BF16),
        compiler_params=_cparams(("arbitrary", "arbitrary")),
        name="attn_paged",
    )(page_table, q_lat, q_pe, ckv_new, kpe_new, cache_ckv, cache_kpe_t)


def _rwkv_kernel(r_ref, k_ref, v_ref, misc_ref, shift_ref, s0_ref,
                 mu_ref, w0_ref, a0_ref, kkw_ref, kaw_ref, rk_ref, gng_ref, gnb_ref,
                 w2_ref, a2_ref, g2_ref,
                 o_ref, sout_ref,
                 prev_ref, st_ref, wl_ref, rt_ref, kt_ref, vv_ref, at_ref, bt_ref, bon_ref, gate_ref,
                 y_ref, *, chunk, heads, head_group):
    c = pl.program_id(1)
    nc = pl.num_programs(1)
    d = r_ref.shape[1]
    misc_w = misc_ref.shape[1]
    lora_w = w2_ref.shape[0]

    @pl.when(c == 0)
    def _():
        prev_ref[...] = shift_ref[0]
        st_ref[...] = s0_ref[0]

    row0 = lax.broadcasted_iota(jnp.int32, (chunk, 1), 0) == 0

    def mix(z, lo, width):
        prev = jnp.where(row0, prev_ref[:, lo:lo + width], pltpu.roll(z, 1, 0))
        return z + (prev - z) * mu_ref[:, lo:lo + width]

    r_raw = r_ref[...]
    k_raw = k_ref[...]
    v_raw = v_ref[...]
    m_raw = misc_ref[...]
    r = mix(r_raw, 0, d)
    k = mix(k_raw, d, d)
    v = mix(v_raw, 2 * d, d)
    m = mix(m_raw, 3 * d, misc_w)
    last = chunk - 1
    prev_ref[:, 0:d] = r_raw[last:last + 1]
    prev_ref[:, d:2 * d] = k_raw[last:last + 1]
    prev_ref[:, 2 * d:3 * d] = v_raw[last:last + 1]
    prev_ref[:, 3 * d:3 * d + misc_w] = m_raw[last:last + 1]

    g_lo = m[:, 0:GATE_LORA]
    lo = m[:, GATE_LORA:GATE_LORA + lora_w]
    w_pre = w0_ref[...] + _dot(jnp.tanh(lo), w2_ref[...])
    a = _sigmoid(a0_ref[...] + _dot(lo, a2_ref[...]))
    gate = _dot(_sigmoid(g_lo), g2_ref[...])
    lw = -DECAY_SCALE * _sigmoid(w_pre)
    ti = lax.broadcasted_iota(jnp.int32, (chunk, chunk), 0)
    tj = lax.broadcasted_iota(jnp.int32, (chunk, chunk), 1)
    incl = ti >= tj
    strict = ti > tj
    tri = incl.astype(BF16)
    hi = lw.astype(BF16)
    mid = (lw - hi.astype(F32)).astype(BF16)
    low = (lw - hi.astype(F32) - mid.astype(F32)).astype(BF16)
    cs = (jnp.dot(tri, hi, preferred_element_type=F32) + jnp.dot(tri, mid, preferred_element_type=F32)
          + jnp.dot(tri, low, preferred_element_type=F32))
    gi_ = lax.broadcasted_iota(jnp.int32, (SEG_BLK, SEG_BLK), 0) // RW_HEAD
    gj_ = lax.broadcasted_iota(jnp.int32, (SEG_BLK, SEG_BLK), 1) // RW_HEAD
    seg = (gi_ == gj_).astype(BF16)

    def head_sum(x):
        x_hi = x.astype(BF16).astype(F32)
        x_lo = x - x_hi
        cols = range(0, d, SEG_BLK)
        stacked = jnp.concatenate([x_hi[:, j:j + SEG_BLK] for j in cols]
                                  + [x_lo[:, j:j + SEG_BLK] for j in cols], axis=0)
        sums = jnp.dot(stacked.astype(BF16), seg, preferred_element_type=F32)
        nblk = len(cols)
        return jnp.concatenate(
            [sums[j * chunk:(j + 1) * chunk] + sums[(nblk + j) * chunk:(nblk + j + 1) * chunk]
             for j in range(nblk)], axis=1)

    w_inv = jnp.exp(-cs)
    kp = k * (1.0 + (a - 1.0) * kaw_ref[...])
    kkr = k * kkw_ref[...]
    kk = kkr * lax.rsqrt(jnp.maximum(head_sum(kkr * kkr), 1e-24))
    w_c = jnp.exp(cs)
    rt_ref[...] = r * w_c
    kt_ref[...] = kp * w_inv
    vv_ref[...] = v
    at_ref[...] = -kk * jnp.exp(cs - lw)
    bt_ref[...] = kk * a * w_inv
    wl_ref[...] = w_c[last:last + 1]
    bon_ref[...] = head_sum(r * kp * rk_ref[...])
    gate_ref[...] = gate

    n_double = max(1, (chunk - 1).bit_length())
    gw = head_group * RW_HEAD

    def group_body(gi, carry):
        off = pl.multiple_of(gi * gw, gw)
        sl = pl.ds(off, gw)
        rt_g = rt_ref[:, sl]
        kt_g = kt_ref[:, sl]
        v_g = vv_ref[:, sl]
        at_g = at_ref[:, sl]
        bt_g = bt_ref[:, sl]
        wl_g = wl_ref[:, sl]
        s0s = [st_ref[gi * head_group + i] for i in range(head_group)]
        hrange = range(head_group)
        hsl = [slice(i * RW_HEAD, (i + 1) * RW_HEAD) for i in hrange]
        vh = [v_g[:, hs] for hs in hsl]
        lhs = [jnp.concatenate([at_g[:, hs], rt_g[:, hs]], axis=0) for hs in hsl]
        rhs = [jnp.concatenate([bt_g[:, hs], kt_g[:, hs]], axis=0) for hs in hsl]
        gram = [_dot_nt(lhs[i], rhs[i]) for i in hrange]
        h0 = [_dot_nt(lhs[i], s0s[i]) for i in hrange]
        pw = [jnp.where(strict, gram[i][:chunk, :chunk], 0.0) for i in hrange]
        u = [h0[i][:chunk] + _dot(jnp.where(strict, gram[i][:chunk, chunk:], 0.0), vh[i]) for i in hrange]
        for step in range(n_double):
            u = [u[i] + _dot(pw[i], u[i]) for i in hrange]
            if step + 1 < n_double:
                pw = [_dot(pw[i], pw[i]) for i in hrange]
        uv = [jnp.concatenate([u[i], vh[i]], axis=0) for i in hrange]
        ys = []
        for i in hrange:
            a_r = jnp.concatenate([jnp.where(incl, gram[i][chunk:, :chunk], 0.0),
                                   jnp.where(incl, gram[i][chunk:, chunk:], 0.0)], axis=1)
            ys.append(h0[i][chunk:] + _dot(a_r, uv[i]))
        s_new = [(s0s[i] + _dot_tn(uv[i], rhs[i])) * wl_g[:, hsl[i]] for i in hrange]
        y_ref[:, sl] = jnp.concatenate(ys, axis=-1)
        for i in range(head_group):
            st_ref[gi * head_group + i] = s_new[i]
        return carry

    lax.fori_loop(0, heads // head_group, group_body, 0)
    y = y_ref[...]
    yc = y - head_sum(y) * (1.0 / RW_HEAD)
    var = head_sum(yc * yc) * (1.0 / RW_HEAD)
    yn = yc * lax.rsqrt(var + GN_EPS) * gng_ref[...] + gnb_ref[...]
    o_ref[...] = (yn + bon_ref[...] * vv_ref[...]) * gate_ref[...]

    @pl.when(c == nc - 1)
    def _():
        sout_ref[0] = st_ref[...]


def _rwkv(zp, misc_block, shift_perm, s0, vecs, w2p, a2p, g2, *, batch, seq, chunk, head_group):
    d = g2.shape[1]
    heads = d // RW_HEAD
    nc = seq // chunk
    misc_w = IN_BLK
    row = lambda b, c: b * nc + c
    full = lambda shape: pl.BlockSpec(shape, lambda b, c: (0,) * len(shape))
    mu, w0, a0, kkw, kaw, rk, gng, gnb = vecs
    big = pltpu.VMEM((chunk, d), F32)
    return pl.pallas_call(
        functools.partial(_rwkv_kernel, chunk=chunk, heads=heads, head_group=head_group),
        grid=(batch, nc),
        in_specs=[
            pl.BlockSpec((chunk, d), lambda b, c: (row(b, c), 0)),
            pl.BlockSpec((chunk, d), lambda b, c: (row(b, c), 1)),
            pl.BlockSpec((chunk, d), lambda b, c: (row(b, c), 2)),
            pl.BlockSpec((chunk, misc_w), lambda b, c: (row(b, c), misc_block)),
            pl.BlockSpec((1, 1, 3 * d + misc_w), lambda b, c: (b, 0, 0)),
            pl.BlockSpec((1, heads, RW_HEAD, RW_HEAD), lambda b, c: (b, 0, 0, 0)),
            full((1, 3 * d + misc_w)),
            full((1, d)), full((1, d)), full((1, d)), full((1, d)), full((1, d)), full((1, d)), full((1, d)),
            full(w2p.shape), full(a2p.shape), full(g2.shape),
        ],
        out_specs=[
            pl.BlockSpec((chunk, d), lambda b, c: (row(b, c), 0)),
            pl.BlockSpec((1, heads, RW_HEAD, RW_HEAD), lambda b, c: (b, 0, 0, 0)),
        ],
        out_shape=[
            jax.ShapeDtypeStruct((batch * seq, d), F32),
            jax.ShapeDtypeStruct((batch, heads, RW_HEAD, RW_HEAD), F32),
        ],
        scratch_shapes=[
            pltpu.VMEM((1, 3 * d + misc_w), F32),
            pltpu.VMEM((heads, RW_HEAD, RW_HEAD), F32),
            pltpu.VMEM((1, d), F32),
            big, big, big, big, big, big, big, big,
        ],
        compiler_params=_cparams(("arbitrary", "arbitrary")),
        name="rwkv",
    )(zp, zp, zp, zp, shift_perm, s0, mu, w0, a0, kkw, kaw, rk, gng, gnb, w2p, a2p, g2)


def _layer_norm(x, g, b):
    mu = jnp.mean(x, axis=-1, keepdims=True)
    xc = x - mu
    var = jnp.mean(xc * xc, axis=-1, keepdims=True)
    return xc * lax.rsqrt(var + LN_EPS) * g + b


def _out_proj_kernel(gm_ref, gr_ref, om_ref, or_ref, x_ref, w_ref, g_ref, b_ref, o_ref, *, alpha):
    merged = gm_ref[...] * om_ref[...] + gr_ref[...] * or_ref[...]
    y = alpha * x_ref[...] + _dot(merged, w_ref[...])
    o_ref[...] = _layer_norm(y, g_ref[...], b_ref[...])


def _out_proj(zp, o_mla, o_rw, x, w_o, ln_g, ln_b, *, alpha, tm):
    n, d = x.shape
    rows = lambda blk: pl.BlockSpec((tm, d), lambda i: (i, blk))
    const = lambda shape: pl.BlockSpec(shape, lambda i: (0, 0))
    return pl.pallas_call(
        functools.partial(_out_proj_kernel, alpha=alpha),
        grid=(n // tm,),
        in_specs=[rows(3), rows(4), rows(0), rows(0), rows(0), const((d, d)), const((1, d)), const((1, d))],
        out_specs=rows(0),
        out_shape=jax.ShapeDtypeStruct((n, d), F32),
        compiler_params=_cparams(("arbitrary",)),
        name="out_proj",
    )(zp, zp, o_mla, o_rw, x, w_o, ln_g, ln_b)


def _ffn_kernel(h_ref, wg_ref, wu_ref, wd_ref, g_ref, b_ref, o_ref, hb_ref, acc_ref, *, alpha):
    j = pl.program_id(1)

    @pl.when(j == 0)
    def _():
        hb_ref[...] = h_ref[...].astype(BF16)
        acc_ref[...] = jnp.zeros(acc_ref.shape, F32)

    hb = hb_ref[...]
    gate = jnp.dot(hb, wg_ref[...], preferred_element_type=F32)
    up = jnp.dot(hb, wu_ref[...], preferred_element_type=F32)
    acc_ref[...] += _dot(gate * _sigmoid(gate) * up, wd_ref[...])

    @pl.when(j == pl.num_programs(1) - 1)
    def _():
        o_ref[...] = _layer_norm(alpha * h_ref[...] + acc_ref[...], g_ref[...], b_ref[...])


def _ffn(h, wg, wu, wd, ln_g, ln_b, *, alpha, tm, tf):
    n, d = h.shape
    d_ff = wg.shape[1]
    return pl.pallas_call(
        functools.partial(_ffn_kernel, alpha=alpha),
        grid=(n // tm, d_ff // tf),
        in_specs=[
            pl.BlockSpec((tm, d), lambda i, j: (i, 0)),
            pl.BlockSpec((d, tf), lambda i, j: (0, j)),
            pl.BlockSpec((d, tf), lambda i, j: (0, j)),
            pl.BlockSpec((tf, d), lambda i, j: (j, 0)),
            pl.BlockSpec((1, d), lambda i, j: (0, 0)),
            pl.BlockSpec((1, d), lambda i, j: (0, 0)),
        ],
        out_specs=pl.BlockSpec((tm, d), lambda i, j: (i, 0)),
        out_shape=jax.ShapeDtypeStruct((n, d), F32),
        scratch_shapes=[pltpu.VMEM((tm, d), BF16), pltpu.VMEM((tm, d), F32)],
        compiler_params=_cparams(("arbitrary", "arbitrary")),
        name="ffn",
    )(h, wg, wu, wd, ln_g, ln_b)


def _ple_kernel(h_ref, wpg_ref, pe_ref, wpe_ref, o_ref, hb_ref, *, tn):
    j = pl.program_id(1)

    @pl.when(j == 0)
    def _():
        hb_ref[...] = h_ref[...].astype(BF16)

    gate = _sigmoid(jnp.dot(hb_ref[...], wpg_ref[...], preferred_element_type=F32))
    emb = _dot(pe_ref[...], wpe_ref[...])
    o_ref[...] = h_ref[:, pl.ds(pl.multiple_of(j * tn, tn), tn)] + gate * emb


def _ple(h, w_pg, pe, w_pe, *, tm, tn):
    n, d = h.shape
    p = pe.shape[1]
    return pl.pallas_call(
        functools.partial(_ple_kernel, tn=tn),
        grid=(n // tm, d // tn),
        in_specs=[
            pl.BlockSpec((tm, d), lambda i, j: (i, 0)),
            pl.BlockSpec((d, tn), lambda i, j: (0, j)),
            pl.BlockSpec((tm, p), lambda i, j: (i, 0)),
            pl.BlockSpec((p, tn), lambda i, j: (0, j)),
        ],
        out_specs=pl.BlockSpec((tm, tn), lambda i, j: (i, j)),
        out_shape=jax.ShapeDtypeStruct((n, d), F32),
        scratch_shapes=[pltpu.VMEM((tm, d), BF16)],
        compiler_params=_cparams(("arbitrary", "arbitrary")),
        name="ple",
    )(h, w_pg, pe, w_pe)


def _rope_tables(pos):
    half = QK_ROPE // 2
    inv = ROPE_THETA ** (-jnp.arange(half, dtype=F32) / half)
    ang = pos[:, None] * inv[None, :]
    cos = jnp.cos(ang)
    sin = jnp.sin(ang)
    return jnp.concatenate([cos, cos], axis=-1), jnp.concatenate([-sin, sin], axis=-1)


def _tile_rows(t, tm):
    reps = max(1, tm // t.shape[0])
    return jnp.tile(t, (reps, 1)) if reps > 1 else t


def kernel(x_prompt, x_sample, p_prompt, p_sample, cache_ckv, cache_kpe, state_wkv, state_shift, page_table, w_in, mu_shift, g_q, g_kv, w_uq, w_uk, w_uv, rw_w0, rw_w2, rw_a0, rw_a2, rw_g2, rw_kk, rw_ka, rw_rk, gn_g, gn_b, w_o, ln1_g, ln1_b, w_ffn_gate, w_ffn_up, w_ffn_down, ln2_g, ln2_b, w_ple, w_ple_gate):
    depth = w_in.shape[0]
    assert depth == 1, "single-layer trunk"
    bp, tp, d = x_prompt.shape
    bs, ts, _ = x_sample.shape
    q_lora = g_q.shape[1]
    kv_lora = g_kv.shape[1]
    page = cache_ckv.shape[2]
    past_len = page_table.shape[1] * page
    rw_cols = mu_shift.shape[1]
    mla_cols = q_lora + kv_lora + QK_ROPE
    off_gate = mla_cols + rw_cols
    alpha = (2.0 * depth) ** 0.25
    assert d // RW_HEAD * RW_HEAD == d and q_lora == IN_BLK and kv_lora == IN_BLK
    assert GATE_LORA + DECAY_LORA + AAA_LORA + QK_ROPE == IN_BLK and 2 * QK_ROPE == LANES
    assert QK_NOPE == V_HEAD

    o_r, o_wlo, o_k, o_v = 0, d, d + DECAY_LORA, 2 * d + DECAY_LORA
    o_alo = 3 * d + DECAY_LORA
    o_glo = o_alo + AAA_LORA

    def rw_perm(t):
        return jnp.concatenate([
            t[..., o_r:o_r + d], t[..., o_k:o_k + d], t[..., o_v:o_v + d],
            t[..., o_glo:o_glo + GATE_LORA], t[..., o_wlo:o_wlo + DECAY_LORA],
            t[..., o_alo:o_alo + AAA_LORA]], axis=-1)

    wi = w_in[0].astype(BF16)
    w_rw = rw_perm(wi[:, mla_cols:off_gate])
    w_perm = jnp.concatenate([
        w_rw[:, :3 * d], wi[:, off_gate:], wi[:, :q_lora + kv_lora],
        w_rw[:, 3 * d:], wi[:, q_lora + kv_lora:mla_cols]], axis=-1)
    cq_block = 5 * d // IN_BLK
    ckv_block = cq_block + 1
    misc_block = cq_block + 2
    ones = jnp.ones((1, IN_BLK), F32)
    gains = jnp.concatenate(
        [jnp.ones((1, 5 * d), F32), g_q[0][None], g_kv[0][None], ones], axis=-1)
    zeros_kpe = jnp.zeros((1, QK_ROPE), F32)
    mu_perm = jnp.concatenate([rw_perm(mu_shift[0])[None], zeros_kpe], axis=-1)

    wq = w_uq[0]
    w_uq_perm = jnp.concatenate([
        wq[:, :, :QK_NOPE].reshape(q_lora, MLA_HEADS * QK_NOPE),
        wq[:, :, QK_NOPE:].reshape(q_lora, MLA_HEADS * QK_ROPE)], axis=-1).astype(BF16)
    w_kv = jnp.concatenate([
        w_uk[0].reshape(kv_lora, MLA_HEADS * QK_NOPE),
        w_uv[0].reshape(kv_lora, MLA_HEADS * V_HEAD)], axis=-1).astype(BF16)

    lora_w = IN_BLK - GATE_LORA
    w2p = jnp.zeros((lora_w, d), F32).at[:DECAY_LORA].set(rw_w2[0]).astype(BF16)
    a2p = jnp.zeros((lora_w, d), F32).at[DECAY_LORA:DECAY_LORA + AAA_LORA].set(rw_a2[0]).astype(BF16)
    g2 = rw_g2[0].astype(BF16)
    vecs = (mu_perm, rw_w0, rw_a0, rw_kk, rw_ka, rw_rk[0].reshape(1, d), gn_g, gn_b)

    w_o_b = w_o[0].astype(BF16)
    wg_b = w_ffn_gate[0].astype(BF16)
    wu_b = w_ffn_up[0].astype(BF16)
    wd_b = w_ffn_down[0].astype(BF16)
    w_pg_b = w_ple_gate[0].astype(BF16)
    w_pe_b = w_ple[0].astype(BF16)

    def trunk(x3, pe3, pos, shift_prev, wkv_prev, chunk, head_group, attend):
        b, t, _ = x3.shape
        n = b * t
        x = x3.reshape(n, d)
        tm = min(1024, n)
        cos64, sin64 = _rope_tables(pos)
        pad = IN_BLK - QK_ROPE
        c_misc = _tile_rows(jnp.concatenate([jnp.ones_like(cos64), cos64], axis=1), tm)
        s_misc = _tile_rows(jnp.concatenate([jnp.zeros_like(sin64), sin64], axis=1), tm)
        zp = _in_proj(x, w_perm, gains, c_misc, s_misc, d_model=d, tm=tm)
        ckv = zp[:, ckv_block * IN_BLK:(ckv_block + 1) * IN_BLK]
        kpe = zp[:, misc_block * IN_BLK + pad:]
        tmq = min(1024, n)
        c_q = _tile_rows(jnp.concatenate([cos64, cos64], axis=1), tmq)
        s_q = _tile_rows(jnp.concatenate([sin64, sin64], axis=1), tmq)
        q = _q_proj(zp, cq_block, w_uq_perm, c_q, s_q, tm=tmq)
        o_mla = attend(zp, q, ckv, kpe)
        shift_perm = jnp.concatenate(
            [rw_perm(shift_prev), jnp.zeros((b, QK_ROPE), F32)], axis=-1)[:, None, :]
        o_rw, wkv_new = _rwkv(zp, misc_block, shift_perm, wkv_prev, vecs, w2p, a2p, g2,
                              batch=b, seq=t, chunk=chunk, head_group=head_group)
        h1 = _out_proj(zp, o_mla, o_rw, x, w_o_b, ln1_g, ln1_b, alpha=alpha, tm=min(256, n))
        h2 = _ffn(h1, wg_b, wu_b, wd_b, ln2_g, ln2_b, alpha=alpha, tm=min(512, n), tf=512)
        out = _ple(h2, w_pg_b, pe3.reshape(n, -1), w_pe_b, tm=min(1024, n), tn=1024)
        last = zp.reshape(b, t, -1)[:, -1]
        rw_last = jnp.concatenate([last[:, :3 * d], last[:, misc_block * IN_BLK:misc_block * IN_BLK + pad]], -1)
        shift_new = jnp.concatenate([
            rw_last[:, 0:d], rw_last[:, 3 * d + GATE_LORA:3 * d + GATE_LORA + DECAY_LORA],
            rw_last[:, d:3 * d], rw_last[:, 3 * d + GATE_LORA + DECAY_LORA:],
            rw_last[:, 3 * d:3 * d + GATE_LORA]], axis=-1)
        return (out.reshape(b, t, d), ckv.reshape(b, t, kv_lora), kpe.reshape(b, t, QK_ROPE),
                shift_new, wkv_new)

    def attend_prompt(zp, q, ckv, kpe):
        n = zp.shape[0]
        kv = _mm(zp, ckv_block, kv_lora, w_kv, tm=min(1024, n), tn=2048, out_dtype=BF16, name="kv_proj")
        return _attn_prompt(q, kv, kpe, batch=bp, seq=tp, tq=min(512, tp))

    n_phys = cache_ckv.shape[1]
    cache_c = cache_ckv.reshape(n_phys, page, kv_lora)
    cache_r = jnp.swapaxes(cache_kpe.reshape(n_phys, page, QK_ROPE), 1, 2)

    def attend_sample(zp, q, ckv, kpe):
        n = zp.shape[0]
        nope_cols = MLA_HEADS * QK_NOPE
        q_lat = _head_mm(q[:, :nope_cols], w_kv, MLA_HEADS, QK_NOPE, 0, contract_w_cols=True,
                         seq_out=(bs, ts), out_dtype=BF16, name="q_lat")
        q_pe = jnp.swapaxes(q[:, nope_cols:].reshape(bs, ts, MLA_HEADS, QK_ROPE), 1, 2)
        q_pe = q_pe.reshape(bs, MLA_HEADS * ts, QK_ROPE)
        o_lat = _attn_paged(page_table, q_lat, q_pe, ckv.reshape(bs, ts, kv_lora),
                            kpe.reshape(bs, ts, QK_ROPE), cache_c, cache_r,
                            pages=min(32, page_table.shape[1]))
        return _head_mm(o_lat, w_kv, MLA_HEADS, V_HEAD, MLA_HEADS, contract_w_cols=False,
                        seq_out=None, out_dtype=F32, name="o_lat")

    pos_p = jnp.arange(tp, dtype=F32)
    pos_s = past_len + jnp.arange(ts, dtype=F32)
    shift0 = jnp.zeros((bp, rw_cols), F32)
    wkv0 = jnp.zeros((bp, d // RW_HEAD, RW_HEAD, RW_HEAD), F32)
    hp, c1, k1, s1, w1 = trunk(x_prompt, p_prompt[0], pos_p, shift0, wkv0, min(PROMPT_CHUNK, tp),
                               RW_GROUP_PROMPT, attend_prompt)
    hs, c2, k2, s2, w2 = trunk(x_sample, p_sample[0], pos_s, state_shift[0], state_wkv[0], ts,
                               RW_GROUP_SAMPLE, attend_sample)
    return (hp, hs, c1[None], k1[None], w1[None], s1[None], c2[None], k2[None], w2[None], s2[None])
```

```python
import functools

import jax
import jax.numpy as jnp
from jax import lax
from jax.experimental import pallas as pl
from jax.experimental.pallas import tpu as pltpu

F32 = jnp.float32
BF16 = jnp.bfloat16

MLA_HEADS = 16
QK_NOPE = 128
QK_ROPE = 64
V_HEAD = 128
RW_HEAD = 64
DECAY_LORA = 96
AAA_LORA = 96
GATE_LORA = 256
ROPE_THETA = 10000.0
SM_SCALE = (QK_NOPE + QK_ROPE) ** -0.5
GN_EPS = 64e-5
LN_EPS = 1e-5
RMS_EPS = 1e-6
DECAY_SCALE = 0.6065306597126334
PROMPT_CHUNK = 64
RW_GROUP_PROMPT = 32
RW_GROUP_SAMPLE = 32

LANES = 128
SUBLANES = 8
SEG_BLK = 256
VMEM_LIMIT_BYTES = 56 * 1024 * 1024

NEG_INF = float("-inf")
KK_NORM_FLOOR = 1e-12


def _tiles(n_rows, seq, n_pages):
    return dict(
        in_proj_rows=min(1024, n_rows),
        q_proj_rows=min(1024, n_rows),
        kv_proj_rows=min(1024, n_rows), kv_proj_cols=2048,
        out_proj_rows=min(256, n_rows),
        ffn_rows=min(512, n_rows), ffn_cols=512,
        ple_rows=min(1024, n_rows), ple_cols=1024,
        attn_q_rows=min(512, seq),
        pages_per_step=min(32, n_pages),
    )


def _cparams(sem, flags=None):
    return pltpu.CompilerParams(dimension_semantics=sem, vmem_limit_bytes=VMEM_LIMIT_BYTES, flags=flags)


def _dot(a, b):
    return jnp.dot(a.astype(BF16), b.astype(BF16), preferred_element_type=F32)


def _dot_nt(a, b):
    return lax.dot_general(a.astype(BF16), b.astype(BF16), (((1,), (1,)), ((), ())),
                           preferred_element_type=F32)


def _dot_tn(a, b):
    return lax.dot_general(a.astype(BF16), b.astype(BF16), (((0,), (0,)), ((), ())),
                           preferred_element_type=F32)


def _sigmoid(x):
    return 0.5 * jnp.tanh(0.5 * x) + 0.5


def _swap_halves(x, half):
    n = x.shape[-1]
    lane = lax.broadcasted_iota(jnp.int32, x.shape, x.ndim - 1)
    first = (lane % (2 * half)) < half
    return jnp.where(first, pltpu.roll(x, n - half, x.ndim - 1), pltpu.roll(x, half, x.ndim - 1))


IN_BLK = 512


def _in_proj_kernel(x_ref, w_ref, g_ref, c_ref, s_ref, o_ref, xb_ref, *, n_raw, n_sig):
    j = pl.program_id(1)

    @pl.when(j == 0)
    def _():
        xb_ref[...] = x_ref[...].astype(BF16)

    z = jnp.dot(xb_ref[...], w_ref[...], preferred_element_type=F32)

    @pl.when(j < n_raw)
    def _():
        o_ref[...] = z

    @pl.when((j >= n_raw) & (j < n_raw + n_sig))
    def _():
        o_ref[...] = _sigmoid(z)

    @pl.when((j >= n_raw + n_sig) & (j < n_raw + n_sig + 2))
    def _():
        ms = jnp.mean(z * z, axis=-1, keepdims=True)
        o_ref[...] = z * lax.rsqrt(ms + RMS_EPS) * g_ref[...]

    @pl.when(j == n_raw + n_sig + 2)
    def _():
        keep = IN_BLK - LANES
        tail = z[:, keep:]
        o_ref[:, :keep] = z[:, :keep]
        o_ref[:, keep:] = tail * c_ref[...] + _swap_halves(tail, QK_ROPE // 2) * s_ref[...]


def _in_proj(x, w_perm, gains, ctab, stab, *, d_model, tm):
    n = x.shape[0]
    cols = w_perm.shape[1]
    nj = cols // IN_BLK
    n_raw = 3 * d_model // IN_BLK
    n_sig = 2 * d_model // IN_BLK
    assert nj == n_raw + n_sig + 3
    ntab = ctab.shape[0] // tm
    return pl.pallas_call(
        functools.partial(_in_proj_kernel, n_raw=n_raw, n_sig=n_sig),
        grid=(n // tm, nj),
        in_specs=[
            pl.BlockSpec((tm, d_model), lambda i, j: (i, 0)),
            pl.BlockSpec((d_model, IN_BLK), lambda i, j: (0, j)),
            pl.BlockSpec((1, IN_BLK), lambda i, j: (0, j)),
            pl.BlockSpec((tm, LANES), lambda i, j: (i % ntab, 0)),
            pl.BlockSpec((tm, LANES), lambda i, j: (i % ntab, 0)),
        ],
        out_specs=pl.BlockSpec((tm, IN_BLK), lambda i, j: (i, j)),
        out_shape=jax.ShapeDtypeStruct((n, cols), F32),
        scratch_shapes=[pltpu.VMEM((tm, d_model), BF16)],
        compiler_params=_cparams(("arbitrary", "arbitrary")),
        name="in_proj",
    )(x, w_perm, gains, ctab, stab)


def _q_proj_kernel(cq_ref, w_ref, c_ref, s_ref, o_ref, *, n_nope_blocks):
    j = pl.program_id(1)
    z = _dot(cq_ref[...], w_ref[...])

    @pl.when(j < n_nope_blocks)
    def _():
        o_ref[...] = z.astype(o_ref.dtype)

    @pl.when(j >= n_nope_blocks)
    def _():
        reps = z.shape[1] // LANES
        c = jnp.concatenate([c_ref[...]] * reps, axis=1)
        s = jnp.concatenate([s_ref[...]] * reps, axis=1)
        o_ref[...] = (z * c + _swap_halves(z, QK_ROPE // 2) * s).astype(o_ref.dtype)


def _q_proj(zp, cq_block, w_uq_perm, ctab, stab, *, tm):
    n = zp.shape[0]
    q_lora = w_uq_perm.shape[0]
    cols = w_uq_perm.shape[1]
    tn = MLA_HEADS * QK_ROPE
    ntab = ctab.shape[0] // tm
    return pl.pallas_call(
        functools.partial(_q_proj_kernel, n_nope_blocks=MLA_HEADS * QK_NOPE // tn),
        grid=(n // tm, cols // tn),
        in_specs=[
            pl.BlockSpec((tm, q_lora), lambda i, j: (i, cq_block)),
            pl.BlockSpec((q_lora, tn), lambda i, j: (0, j)),
            pl.BlockSpec((tm, LANES), lambda i, j: (i % ntab, 0)),
            pl.BlockSpec((tm, LANES), lambda i, j: (i % ntab, 0)),
        ],
        out_specs=pl.BlockSpec((tm, tn), lambda i, j: (i, j)),
        out_shape=jax.ShapeDtypeStruct((n, cols), BF16),
        compiler_params=_cparams(("arbitrary", "arbitrary")),
        name="q_proj",
    )(zp, w_uq_perm, ctab, stab)


def _mm_kernel(x_ref, w_ref, o_ref):
    o_ref[...] = _dot(x_ref[...], w_ref[...]).astype(o_ref.dtype)


def _mm(x, x_block, k, w, *, tm, tn, out_dtype, name):
    n = x.shape[0]
    cols = w.shape[1]
    return pl.pallas_call(
        _mm_kernel,
        grid=(n // tm, cols // tn),
        in_specs=[
            pl.BlockSpec((tm, k), lambda i, j: (i, x_block)),
            pl.BlockSpec((k, tn), lambda i, j: (0, j)),
        ],
        out_specs=pl.BlockSpec((tm, tn), lambda i, j: (i, j)),
        out_shape=jax.ShapeDtypeStruct((n, cols), out_dtype),
        compiler_params=_cparams(("arbitrary", "arbitrary")),
        name=name,
    )(x, w)


def _head_mm_kernel(x_ref, w_ref, o_ref, *, contract_w_cols):
    dot = _dot_nt if contract_w_cols else _dot
    x = x_ref[...]
    if x.ndim == 3:
        x = x.astype(F32).reshape(x.shape[0] * x.shape[1], x.shape[2])
    res = dot(x, w_ref[...])
    o_ref[...] = res.reshape(o_ref.shape).astype(o_ref.dtype)


def _head_mm(x, w, heads, c, w_block0, *, contract_w_cols, seq_out, out_dtype, name):
    r = w.shape[0]
    m = r if contract_w_cols else c
    if x.ndim == 3:
        nseq, rows, k = x.shape
        t = rows // heads
        n = nseq * t
        x_spec = pl.BlockSpec((nseq, t, k), lambda i: (0, i, 0))
    else:
        n = x.shape[0]
        x_spec = pl.BlockSpec((n, x.shape[1] // heads), lambda i: (0, i))
    if seq_out is None:
        out_spec = pl.BlockSpec((n, m), lambda i: (0, i))
        out_shape = jax.ShapeDtypeStruct((n, heads * m), out_dtype)
    else:
        nseq, t = seq_out
        out_spec = pl.BlockSpec((nseq, t, m), lambda i: (0, i, 0))
        out_shape = jax.ShapeDtypeStruct((nseq, heads * t, m), out_dtype)
    return pl.pallas_call(
        functools.partial(_head_mm_kernel, contract_w_cols=contract_w_cols),
        grid=(heads,),
        in_specs=[x_spec, pl.BlockSpec((r, c), lambda i: (0, w_block0 + i))],
        out_specs=out_spec,
        out_shape=out_shape,
        compiler_params=_cparams(("arbitrary",)),
        name=name,
    )(x, w)


HEADS_PER_STEP = 8
ROW_SPLIT = 2


def _softmax_update(s, m_prev, l_prev):
    m_next = jnp.maximum(m_prev, jnp.max(s, axis=-1, keepdims=True))
    alpha = jnp.exp(m_prev - m_next)
    p = jnp.exp(s - m_next)
    l_next = alpha * l_prev + jnp.sum(p, axis=-1, keepdims=True)
    return m_next, l_next, alpha, p.astype(BF16)


def _attn_prompt_kernel(qi_ref, ki_ref, qn_ref, qp_ref, kn_ref, v_ref, kpe_ref, o_ref,
                        m_ref, l_ref, acc_ref, *, tq, tk):
    step = pl.program_id(2)
    qi = qi_ref[step]
    ki = ki_ref[step]
    heads = range(HEADS_PER_STEP)

    @pl.when(ki == 0)
    def _():
        m_ref[...] = jnp.full(m_ref.shape, NEG_INF, F32)
        l_ref[...] = jnp.zeros(l_ref.shape, F32)
        acc_ref[...] = jnp.zeros(acc_ref.shape, F32)

    def block(diagonal):
        kpe = kpe_ref[...].astype(BF16)
        rq = tq // ROW_SPLIT
        units = [(h, r * rq) for h in heads for r in range(ROW_SPLIT)]
        s = [(_dot_nt(qn_ref[r0:r0 + rq, h * QK_NOPE:(h + 1) * QK_NOPE],
                      kn_ref[:, h * QK_NOPE:(h + 1) * QK_NOPE])
              + _dot_nt(qp_ref[r0:r0 + rq, h * QK_ROPE:(h + 1) * QK_ROPE], kpe)) * SM_SCALE
             for h, r0 in units]
        if diagonal:
            s = [jnp.where(lax.broadcasted_iota(jnp.int32, (rq, tk), 1)
                           <= lax.broadcasted_iota(jnp.int32, (rq, tk), 0) + r0, su, NEG_INF)
                 for su, (h, r0) in zip(s, units)]
        upd = [_softmax_update(su, m_ref[h, r0:r0 + rq], l_ref[h, r0:r0 + rq])
               for su, (h, r0) in zip(s, units)]
        for (m_next, l_next, alpha, p), (h, r0) in zip(upd, units):
            m_ref[h, r0:r0 + rq] = m_next
            l_ref[h, r0:r0 + rq] = l_next
            acc_ref[h, r0:r0 + rq] = (alpha * acc_ref[h, r0:r0 + rq]
                                      + _dot(p, v_ref[:, h * V_HEAD:(h + 1) * V_HEAD]))

    @pl.when(ki < qi)
    def _():
        block(False)

    @pl.when(ki == qi)
    def _():
        block(True)
        for h in heads:
            o_ref[:, h * V_HEAD:(h + 1) * V_HEAD] = acc_ref[h] / l_ref[h]


def _attn_prompt(q, kv, kpe, *, batch, seq, tq):
    tk = tq
    nq = seq // tq
    qi_tab = jnp.asarray([qi for qi in range(nq) for _ in range(qi + 1)], jnp.int32)
    ki_tab = jnp.asarray([ki for qi in range(nq) for ki in range(qi + 1)], jnp.int32)
    n_pairs = MLA_HEADS // HEADS_PER_STEP
    nope_w = HEADS_PER_STEP * QK_NOPE
    pe_w = HEADS_PER_STEP * QK_ROPE
    v_w = HEADS_PER_STEP * V_HEAD
    pe_off = MLA_HEADS * QK_NOPE // pe_w
    v_off = MLA_HEADS * QK_NOPE // v_w
    grid_spec = pltpu.PrefetchScalarGridSpec(
        num_scalar_prefetch=2,
        grid=(batch, n_pairs, int(qi_tab.shape[0])),
        in_specs=[
            pl.BlockSpec((tq, nope_w), lambda b, hp, s, qt, kt: (b * nq + qt[s], hp)),
            pl.BlockSpec((tq, pe_w), lambda b, hp, s, qt, kt: (b * nq + qt[s], pe_off + hp)),
            pl.BlockSpec((tk, nope_w), lambda b, hp, s, qt, kt: (b * nq + kt[s], hp)),
            pl.BlockSpec((tk, v_w), lambda b, hp, s, qt, kt: (b * nq + kt[s], v_off + hp)),
            pl.BlockSpec((tk, QK_ROPE), lambda b, hp, s, qt, kt: (b * nq + kt[s], 0)),
        ],
        out_specs=pl.BlockSpec((tq, v_w), lambda b, hp, s, qt, kt: (b * nq + qt[s], hp)),
        scratch_shapes=[
            pltpu.VMEM((HEADS_PER_STEP, tq, 1), F32),
            pltpu.VMEM((HEADS_PER_STEP, tq, 1), F32),
            pltpu.VMEM((HEADS_PER_STEP, tq, V_HEAD), F32),
        ],
    )
    return pl.pallas_call(
        functools.partial(_attn_prompt_kernel, tq=tq, tk=tk),
        grid_spec=grid_spec,
        out_shape=jax.ShapeDtypeStruct((batch * seq, MLA_HEADS * V_HEAD), F32),
        compiler_params=_cparams(("arbitrary", "arbitrary", "arbitrary")),
        name="attn_prompt",
    )(qi_tab, ki_tab, q, q, kv, kv, kpe)


PAGE_SLOTS = 3


def _attn_paged_kernel(pt_ref, ql_ref, qp_ref, cn_ref, kn_ref, ckv_hbm, kpet_hbm, o_ref,
                       kbuf, pbuf, sem, m_ref, l_ref, acc_ref, *, pages, page, t_new):
    b = pl.program_id(0)
    g = pl.program_id(1)
    nb = pl.num_programs(0)
    ng = pl.num_programs(1)
    lin = b * ng + g

    groups = [(b, g, lax.rem(lin, PAGE_SLOTS))]
    for _ in range(PAGE_SLOTS - 1):
        pb, pg, ps = groups[-1]
        wrap = pg + 1 == ng
        groups.append((jnp.where(wrap, jnp.where(pb + 1 == nb, 0, pb + 1), pb),
                       jnp.where(wrap, 0, pg + 1),
                       jnp.where(ps + 1 == PAGE_SLOTS, 0, ps + 1)))

    def copies(ahead):
        bb, gg, sl = groups[ahead]
        out = []
        for i in range(pages):
            pid = pt_ref[bb, gg * pages + i]
            out.append(pltpu.make_async_copy(ckv_hbm.at[pid], kbuf.at[sl, i], sem.at[0, sl]))
            out.append(pltpu.make_async_copy(kpet_hbm.at[pid], pbuf.at[sl, i], sem.at[1, sl]))
        return out

    @pl.when(lin == 0)
    def _():
        for n in range(PAGE_SLOTS - 1):
            for k, c in enumerate(copies(n)):
                c.start(priority=(k // 2) % 2)

    @pl.when(g == 0)
    def _():
        m_ref[...] = jnp.full(m_ref.shape, NEG_INF, F32)
        l_ref[...] = jnp.zeros(l_ref.shape, F32)
        acc_ref[...] = jnp.zeros(acc_ref.shape, F32)

    for c in copies(0):
        c.wait()

    slot = groups[0][2]
    ahead = copies(PAGE_SLOTS - 1)
    ql = ql_ref[0]
    qp = qp_ref[0]
    keys, s = [], []
    for i in range(pages):
        ahead[2 * i].start(priority=i % 2)
        ahead[2 * i + 1].start(priority=i % 2)
        keys.append(kbuf[slot, i].astype(BF16))
        s.append(_dot_nt(ql, keys[i]) + _dot(qp, pbuf[slot, i]))
    half = pages // 2
    m_run, l_run, acc = m_ref[...], l_ref[...], acc_ref[...]
    for lo, hi in ((0, half), (half, pages)):
        s_part = jnp.concatenate(s[lo:hi], axis=-1) * SM_SCALE
        m_run, l_run, alpha, p = _softmax_update(s_part, m_run, l_run)
        pv = _dot(p[:, 0:page], keys[lo])
        for i in range(lo + 1, hi):
            pv = pv + _dot(p[:, (i - lo) * page:(i - lo + 1) * page], keys[i])
        acc = alpha * acc + pv
    m_ref[...] = m_run
    l_ref[...] = l_run
    acc_ref[...] = acc

    @pl.when(lin == nb * ng - 1)
    def _():
        for n in range(1, PAGE_SLOTS):
            for c in copies(n):
                c.wait()

    @pl.when(g == ng - 1)
    def _():
        cn = cn_ref[0].astype(BF16)
        rows = ql.shape[0]
        sn = (_dot_nt(ql, cn) + _dot_nt(qp, kn_ref[0])) * SM_SCALE
        t_row = lax.broadcasted_iota(jnp.int32, (rows, t_new), 0) % t_new
        t_col = lax.broadcasted_iota(jnp.int32, (rows, t_new), 1)
        sn = jnp.where(t_col <= t_row, sn, NEG_INF)
        m_prev = m_ref[...]
        m_next = jnp.maximum(m_prev, jnp.max(sn, axis=-1, keepdims=True))
        alpha = jnp.exp(m_prev - m_next)
        p = jnp.exp(sn - m_next)
        l = alpha * l_ref[...] + jnp.sum(p, axis=-1, keepdims=True)
        acc = alpha * acc_ref[...] + _dot(p, cn)
        o_ref[0] = (acc / l).astype(o_ref.dtype)


def _attn_paged(page_table, q_lat, q_pe, ckv_new, kpe_new, cache_ckv, cache_kpe_t, *, pages):
    nseq, rows, kv_lora = q_lat.shape
    t_new = ckv_new.shape[1]
    page = cache_ckv.shape[1]
    n_pages = page_table.shape[1]
    assert n_pages % pages == 0 and pages % 2 == 0
    in_specs = [
        pl.BlockSpec((1, rows, kv_lora), lambda b, g, pt: (b, 0, 0)),
        pl.BlockSpec((1, rows, QK_ROPE), lambda b, g, pt: (b, 0, 0)),
        pl.BlockSpec((1, t_new, kv_lora), lambda b, g, pt: (b, 0, 0)),
        pl.BlockSpec((1, t_new, QK_ROPE), lambda b, g, pt: (b, 0, 0)),
        pl.BlockSpec(memory_space=pl.ANY),
        pl.BlockSpec(memory_space=pl.ANY),
    ]
    grid_spec = pltpu.PrefetchScalarGridSpec(
        num_scalar_prefetch=1,
        grid=(nseq, n_pages // pages),
        in_specs=in_specs,
        out_specs=pl.BlockSpec((1, rows, kv_lora), lambda b, g, pt: (b, 0, 0)),
        scratch_shapes=[
            pltpu.VMEM((PAGE_SLOTS, pages, page, kv_lora), F32),
            pltpu.VMEM((PAGE_SLOTS, pages, QK_ROPE, page), F32),
            pltpu.SemaphoreType.DMA((2, PAGE_SLOTS)),
            pltpu.VMEM((rows, 1), F32),
            pltpu.VMEM((rows, 1), F32),
            pltpu.VMEM((rows, kv_lora), F32),
        ],
    )
    return pl.pallas_call(
        functools.partial(_attn_paged_kernel, pages=pages, page=page, t_new=t_new),
        grid_spec=grid_spec,
        out_shape=jax.ShapeDtypeStruct((nseq, rows, kv_lora), BF16),
        compiler_params=_cparams(("arbitrary", "arbitrary")),
        name="attn_paged",
    )(page_table, q_lat, q_pe, ckv_new, kpe_new, cache_ckv, cache_kpe_t)


def _rwkv_kernel(r_ref, k_ref, v_ref, misc_ref, shift_ref, s0_ref,
                 mu_ref, w0_ref, a0_ref, kkw_ref, kaw_ref, rk_ref, gng_ref, gnb_ref,
                 w2_ref, a2_ref, g2_ref,
                 o_ref, sout_ref,
                 prev_ref, st_ref, wl_ref, rt_ref, kt_ref, vv_ref, at_ref, bt_ref, bon_ref, gate_ref,
                 y_ref, *, chunk, heads, head_group):
    c = pl.program_id(1)
    nc = pl.num_programs(1)
    d = r_ref.shape[1]
    misc_w = misc_ref.shape[1]
    lora_w = w2_ref.shape[0]

    @pl.when(c == 0)
    def _():
        prev_ref[...] = shift_ref[0]
        st_ref[...] = s0_ref[0]

    sub = SUBLANES
    row0 = lax.broadcasted_iota(jnp.int32, (sub, 1), 0) == 0

    def mix(z, lo, width):
        rolled = pltpu.roll(z, 1, 0)
        first = jnp.where(row0, prev_ref[:, lo:lo + width], rolled[:sub])
        prev = first if chunk == sub else jnp.concatenate([first, rolled[sub:]], axis=0)
        return z + (prev - z) * mu_ref[:, lo:lo + width]

    r_raw = r_ref[...]
    k_raw = k_ref[...]
    v_raw = v_ref[...]
    m_raw = misc_ref[...]
    r = mix(r_raw, 0, d)
    k = mix(k_raw, d, d)
    v = mix(v_raw, 2 * d, d)
    m = mix(m_raw, 3 * d, misc_w)
    last = chunk - 1
    prev_ref[:, 0:d] = r_raw[last:last + 1]
    prev_ref[:, d:2 * d] = k_raw[last:last + 1]
    prev_ref[:, 2 * d:3 * d] = v_raw[last:last + 1]
    prev_ref[:, 3 * d:3 * d + misc_w] = m_raw[last:last + 1]

    g_lo = m[:, 0:GATE_LORA]
    lo = m[:, GATE_LORA:GATE_LORA + lora_w]
    w_pre = w0_ref[...] + _dot(jnp.tanh(lo), w2_ref[...])
    a = _sigmoid(a0_ref[...] + _dot(lo, a2_ref[...]))
    gate = _dot(_sigmoid(g_lo), g2_ref[...])
    lw = -DECAY_SCALE * _sigmoid(w_pre)
    ti = lax.broadcasted_iota(jnp.int32, (chunk, chunk), 0)
    tj = lax.broadcasted_iota(jnp.int32, (chunk, chunk), 1)
    incl = ti >= tj
    strict = ti > tj
    tri = incl.astype(BF16)
    hi = lw.astype(BF16)
    mid = (lw - hi.astype(F32)).astype(BF16)
    low = (lw - hi.astype(F32) - mid.astype(F32)).astype(BF16)
    cs = (jnp.dot(tri, hi, preferred_element_type=F32) + jnp.dot(tri, mid, preferred_element_type=F32)
          + jnp.dot(tri, low, preferred_element_type=F32))
    gi_ = lax.broadcasted_iota(jnp.int32, (SEG_BLK, SEG_BLK), 0) // RW_HEAD
    gj_ = lax.broadcasted_iota(jnp.int32, (SEG_BLK, SEG_BLK), 1) // RW_HEAD
    seg = (gi_ == gj_).astype(BF16)

    def head_sum(x):
        x_hi = x.astype(BF16).astype(F32)
        x_lo = x - x_hi
        cols = range(0, d, SEG_BLK)
        stacked = jnp.concatenate([x_hi[:, j:j + SEG_BLK] for j in cols]
                                  + [x_lo[:, j:j + SEG_BLK] for j in cols], axis=0)
        sums = jnp.dot(stacked.astype(BF16), seg, preferred_element_type=F32)
        nblk = len(cols)
        return jnp.concatenate(
            [sums[j * chunk:(j + 1) * chunk] + sums[(nblk + j) * chunk:(nblk + j + 1) * chunk]
             for j in range(nblk)], axis=1)

    w_inv = jnp.exp(-cs)
    kp = k * (1.0 + (a - 1.0) * kaw_ref[...])
    kkr = k * kkw_ref[...]
    kk = kkr * lax.rsqrt(jnp.maximum(head_sum(kkr * kkr), KK_NORM_FLOOR * KK_NORM_FLOOR))
    w_c = jnp.exp(cs)
    rt_ref[...] = r * w_c
    kt_ref[...] = kp * w_inv
    vv_ref[...] = v
    at_ref[...] = -kk * jnp.exp(cs - lw)
    bt_ref[...] = kk * a * w_inv
    wl_ref[...] = w_c[last:last + 1]
    bon_ref[...] = head_sum(r * kp * rk_ref[...])
    gate_ref[...] = gate

    n_double = max(1, (chunk - 1).bit_length())
    gw = head_group * RW_HEAD

    def group_body(gi, carry):
        off = pl.multiple_of(gi * gw, gw)
        sl = pl.ds(off, gw)
        rt_g = rt_ref[:, sl]
        kt_g = kt_ref[:, sl]
        v_g = vv_ref[:, sl]
        at_g = at_ref[:, sl]
        bt_g = bt_ref[:, sl]
        wl_g = wl_ref[:, sl]
        s0s = [st_ref[gi * head_group + i] for i in range(head_group)]
        hrange = range(head_group)
        hsl = [slice(i * RW_HEAD, (i + 1) * RW_HEAD) for i in hrange]
        vh = [v_g[:, hs] for hs in hsl]
        lhs = [jnp.concatenate([at_g[:, hs], rt_g[:, hs]], axis=0) for hs in hsl]
        rhs = [jnp.concatenate([bt_g[:, hs], kt_g[:, hs]], axis=0) for hs in hsl]
        gram = [_dot_nt(lhs[i], rhs[i]) for i in hrange]
        h0 = [_dot_nt(lhs[i], s0s[i]) for i in hrange]
        pw = [jnp.where(strict, gram[i][:chunk, :chunk], 0.0) for i in hrange]
        u = [h0[i][:chunk] + _dot(jnp.where(strict, gram[i][:chunk, chunk:], 0.0), vh[i]) for i in hrange]
        for step in range(n_double):
            u = [u[i] + _dot(pw[i], u[i]) for i in hrange]
            if step + 1 < n_double:
                pw = [_dot(pw[i], pw[i]) for i in hrange]
        uv = [jnp.concatenate([u[i], vh[i]], axis=0) for i in hrange]
        ys = []
        for i in hrange:
            a_r = jnp.concatenate([jnp.where(incl, gram[i][chunk:, :chunk], 0.0),
                                   jnp.where(incl, gram[i][chunk:, chunk:], 0.0)], axis=1)
            ys.append(h0[i][chunk:] + _dot(a_r, uv[i]))
        s_new = [(s0s[i] + _dot_tn(uv[i], rhs[i])) * wl_g[:, hsl[i]] for i in hrange]
        y_ref[:, sl] = jnp.concatenate(ys, axis=-1)
        for i in range(head_group):
            st_ref[gi * head_group + i] = s_new[i]
        return carry

    lax.fori_loop(0, heads // head_group, group_body, 0)
    y = y_ref[...]
    yc = y - head_sum(y) * (1.0 / RW_HEAD)
    var = head_sum(yc * yc) * (1.0 / RW_HEAD)
    yn = yc * lax.rsqrt(var + GN_EPS) * gng_ref[...] + gnb_ref[...]
    o_ref[...] = (yn + bon_ref[...] * vv_ref[...]) * gate_ref[...]

    @pl.when(c == nc - 1)
    def _():
        sout_ref[0] = st_ref[...]


def _rwkv(zp, misc_block, shift_perm, s0, vecs, w2p, a2p, g2, *, batch, seq, chunk, head_group):
    d = g2.shape[1]
    heads = d // RW_HEAD
    nc = seq // chunk
    misc_w = IN_BLK
    row = lambda b, c: b * nc + c
    full = lambda shape: pl.BlockSpec(shape, lambda b, c: (0,) * len(shape))
    mu, w0, a0, kkw, kaw, rk, gng, gnb = vecs
    big = pltpu.VMEM((chunk, d), F32)
    return pl.pallas_call(
        functools.partial(_rwkv_kernel, chunk=chunk, heads=heads, head_group=head_group),
        grid=(batch, nc),
        in_specs=[
            pl.BlockSpec((chunk, d), lambda b, c: (row(b, c), 0)),
            pl.BlockSpec((chunk, d), lambda b, c: (row(b, c), 1)),
            pl.BlockSpec((chunk, d), lambda b, c: (row(b, c), 2)),
            pl.BlockSpec((chunk, misc_w), lambda b, c: (row(b, c), misc_block)),
            pl.BlockSpec((1, 1, 3 * d + misc_w), lambda b, c: (b, 0, 0)),
            pl.BlockSpec((1, heads, RW_HEAD, RW_HEAD), lambda b, c: (b, 0, 0, 0)),
            full((1, 3 * d + misc_w)),
            full((1, d)), full((1, d)), full((1, d)), full((1, d)), full((1, d)), full((1, d)), full((1, d)),
            full(w2p.shape), full(a2p.shape), full(g2.shape),
        ],
        out_specs=[
            pl.BlockSpec((chunk, d), lambda b, c: (row(b, c), 0)),
            pl.BlockSpec((1, heads, RW_HEAD, RW_HEAD), lambda b, c: (b, 0, 0, 0)),
        ],
        out_shape=[
            jax.ShapeDtypeStruct((batch * seq, d), F32),
            jax.ShapeDtypeStruct((batch, heads, RW_HEAD, RW_HEAD), F32),
        ],
        scratch_shapes=[
            pltpu.VMEM((1, 3 * d + misc_w), F32),
            pltpu.VMEM((heads, RW_HEAD, RW_HEAD), F32),
            pltpu.VMEM((1, d), F32),
            big, big, big, big, big, big, big, big,
        ],
        compiler_params=_cparams(("arbitrary", "arbitrary")),
        name="rwkv",
    )(zp, zp, zp, zp, shift_perm, s0, mu, w0, a0, kkw, kaw, rk, gng, gnb, w2p, a2p, g2)


def _layer_norm(x, g, b):
    mu = jnp.mean(x, axis=-1, keepdims=True)
    xc = x - mu
    var = jnp.mean(xc * xc, axis=-1, keepdims=True)
    return xc * lax.rsqrt(var + LN_EPS) * g + b


def _out_proj_kernel(gm_ref, gr_ref, om_ref, or_ref, x_ref, w_ref, g_ref, b_ref, o_ref, *, alpha):
    merged = gm_ref[...] * om_ref[...] + gr_ref[...] * or_ref[...]
    y = alpha * x_ref[...] + _dot(merged, w_ref[...])
    o_ref[...] = _layer_norm(y, g_ref[...], b_ref[...])


def _out_proj(zp, o_mla, o_rw, x, w_o, ln_g, ln_b, *, alpha, tm):
    n, d = x.shape
    rows = lambda blk: pl.BlockSpec((tm, d), lambda i: (i, blk))
    const = lambda shape: pl.BlockSpec(shape, lambda i: (0, 0))
    return pl.pallas_call(
        functools.partial(_out_proj_kernel, alpha=alpha),
        grid=(n // tm,),
        in_specs=[rows(3), rows(4), rows(0), rows(0), rows(0), const((d, d)), const((1, d)), const((1, d))],
        out_specs=rows(0),
        out_shape=jax.ShapeDtypeStruct((n, d), F32),
        compiler_params=_cparams(("arbitrary",)),
        name="out_proj",
    )(zp, zp, o_mla, o_rw, x, w_o, ln_g, ln_b)


def _ffn_kernel(h_ref, wg_ref, wu_ref, wd_ref, g_ref, b_ref, o_ref, hb_ref, acc_ref, *, alpha):
    j = pl.program_id(1)

    @pl.when(j == 0)
    def _():
        hb_ref[...] = h_ref[...].astype(BF16)
        acc_ref[...] = jnp.zeros(acc_ref.shape, F32)

    hb = hb_ref[...]
    gate = jnp.dot(hb, wg_ref[...], preferred_element_type=F32)
    up = jnp.dot(hb, wu_ref[...], preferred_element_type=F32)
    acc_ref[...] += _dot(gate * _sigmoid(gate) * up, wd_ref[...])

    @pl.when(j == pl.num_programs(1) - 1)
    def _():
        o_ref[...] = _layer_norm(alpha * h_ref[...] + acc_ref[...], g_ref[...], b_ref[...])


def _ffn(h, wg, wu, wd, ln_g, ln_b, *, alpha, tm, tf):
    n, d = h.shape
    d_ff = wg.shape[1]
    return pl.pallas_call(
        functools.partial(_ffn_kernel, alpha=alpha),
        grid=(n // tm, d_ff // tf),
        in_specs=[
            pl.BlockSpec((tm, d), lambda i, j: (i, 0)),
            pl.BlockSpec((d, tf), lambda i, j: (0, j)),
            pl.BlockSpec((d, tf), lambda i, j: (0, j)),
            pl.BlockSpec((tf, d), lambda i, j: (j, 0)),
            pl.BlockSpec((1, d), lambda i, j: (0, 0)),
            pl.BlockSpec((1, d), lambda i, j: (0, 0)),
        ],
        out_specs=pl.BlockSpec((tm, d), lambda i, j: (i, 0)),
        out_shape=jax.ShapeDtypeStruct((n, d), F32),
        scratch_shapes=[pltpu.VMEM((tm, d), BF16), pltpu.VMEM((tm, d), F32)],
        compiler_params=_cparams(("arbitrary", "arbitrary")),
        name="ffn",
    )(h, wg, wu, wd, ln_g, ln_b)


def _ple_kernel(h_ref, wpg_ref, pe_ref, wpe_ref, o_ref, hb_ref, *, tn):
    j = pl.program_id(1)

    @pl.when(j == 0)
    def _():
        hb_ref[...] = h_ref[...].astype(BF16)

    gate = _sigmoid(jnp.dot(hb_ref[...], wpg_ref[...], preferred_element_type=F32))
    emb = _dot(pe_ref[...], wpe_ref[...])
    o_ref[...] = h_ref[:, pl.ds(pl.multiple_of(j * tn, tn), tn)] + gate * emb


def _ple(h, w_pg, pe, w_pe, *, tm, tn):
    n, d = h.shape
    p = pe.shape[1]
    return pl.pallas_call(
        functools.partial(_ple_kernel, tn=tn),
        grid=(n // tm, d // tn),
        in_specs=[
            pl.BlockSpec((tm, d), lambda i, j: (i, 0)),
            pl.BlockSpec((d, tn), lambda i, j: (0, j)),
            pl.BlockSpec((tm, p), lambda i, j: (i, 0)),
            pl.BlockSpec((p, tn), lambda i, j: (0, j)),
        ],
        out_specs=pl.BlockSpec((tm, tn), lambda i, j: (i, j)),
        out_shape=jax.ShapeDtypeStruct((n, d), F32),
        scratch_shapes=[pltpu.VMEM((tm, d), BF16)],
        compiler_params=_cparams(("arbitrary", "arbitrary")),
        name="ple",
    )(h, w_pg, pe, w_pe)


def _rope_tables(pos):
    half = QK_ROPE // 2
    inv = ROPE_THETA ** (-jnp.arange(half, dtype=F32) / half)
    ang = pos[:, None] * inv[None, :]
    cos = jnp.cos(ang)
    sin = jnp.sin(ang)
    return jnp.concatenate([cos, cos], axis=-1), jnp.concatenate([-sin, sin], axis=-1)


def _tile_rows(t, tm):
    reps = max(1, tm // t.shape[0])
    return jnp.tile(t, (reps, 1)) if reps > 1 else t


def kernel(x_prompt, x_sample, p_prompt, p_sample, cache_ckv, cache_kpe, state_wkv, state_shift, page_table, w_in, mu_shift, g_q, g_kv, w_uq, w_uk, w_uv, rw_w0, rw_w2, rw_a0, rw_a2, rw_g2, rw_kk, rw_ka, rw_rk, gn_g, gn_b, w_o, ln1_g, ln1_b, w_ffn_gate, w_ffn_up, w_ffn_down, ln2_g, ln2_b, w_ple, w_ple_gate):
    depth = w_in.shape[0]
    assert depth == 1, "single-layer trunk"
    bp, tp, d = x_prompt.shape
    bs, ts, _ = x_sample.shape
    q_lora = g_q.shape[1]
    kv_lora = g_kv.shape[1]
    page = cache_ckv.shape[2]
    past_len = page_table.shape[1] * page
    rw_cols = mu_shift.shape[1]
    mla_cols = q_lora + kv_lora + QK_ROPE
    off_gate = mla_cols + rw_cols
    alpha = (2.0 * depth) ** 0.25
    assert d // RW_HEAD * RW_HEAD == d and q_lora == IN_BLK and kv_lora == IN_BLK
    assert GATE_LORA + DECAY_LORA + AAA_LORA + QK_ROPE == IN_BLK and 2 * QK_ROPE == LANES
    assert QK_NOPE == V_HEAD

    o_r, o_wlo, o_k, o_v = 0, d, d + DECAY_LORA, 2 * d + DECAY_LORA
    o_alo = 3 * d + DECAY_LORA
    o_glo = o_alo + AAA_LORA

    def rw_perm(t):
        return jnp.concatenate([
            t[..., o_r:o_r + d], t[..., o_k:o_k + d], t[..., o_v:o_v + d],
            t[..., o_glo:o_glo + GATE_LORA], t[..., o_wlo:o_wlo + DECAY_LORA],
            t[..., o_alo:o_alo + AAA_LORA]], axis=-1)

    wi = w_in[0].astype(BF16)
    w_rw = rw_perm(wi[:, mla_cols:off_gate])
    w_perm = jnp.concatenate([
        w_rw[:, :3 * d], wi[:, off_gate:], wi[:, :q_lora + kv_lora],
        w_rw[:, 3 * d:], wi[:, q_lora + kv_lora:mla_cols]], axis=-1)
    cq_block = 5 * d // IN_BLK
    ckv_block = cq_block + 1
    misc_block = cq_block + 2
    ones = jnp.ones((1, IN_BLK), F32)
    gains = jnp.concatenate(
        [jnp.ones((1, 5 * d), F32), g_q[0][None], g_kv[0][None], ones], axis=-1)
    zeros_kpe = jnp.zeros((1, QK_ROPE), F32)
    mu_perm = jnp.concatenate([rw_perm(mu_shift[0])[None], zeros_kpe], axis=-1)

    wq = w_uq[0]
    w_uq_perm = jnp.concatenate([
        wq[:, :, :QK_NOPE].reshape(q_lora, MLA_HEADS * QK_NOPE),
        wq[:, :, QK_NOPE:].reshape(q_lora, MLA_HEADS * QK_ROPE)], axis=-1).astype(BF16)
    w_kv = jnp.concatenate([
        w_uk[0].reshape(kv_lora, MLA_HEADS * QK_NOPE),
        w_uv[0].reshape(kv_lora, MLA_HEADS * V_HEAD)], axis=-1).astype(BF16)

    lora_w = IN_BLK - GATE_LORA
    w2p = jnp.zeros((lora_w, d), F32).at[:DECAY_LORA].set(rw_w2[0]).astype(BF16)
    a2p = jnp.zeros((lora_w, d), F32).at[DECAY_LORA:DECAY_LORA + AAA_LORA].set(rw_a2[0]).astype(BF16)
    g2 = rw_g2[0].astype(BF16)
    vecs = (mu_perm, rw_w0, rw_a0, rw_kk, rw_ka, rw_rk[0].reshape(1, d), gn_g, gn_b)

    w_o_b = w_o[0].astype(BF16)
    wg_b = w_ffn_gate[0].astype(BF16)
    wu_b = w_ffn_up[0].astype(BF16)
    wd_b = w_ffn_down[0].astype(BF16)
    w_pg_b = w_ple_gate[0].astype(BF16)
    w_pe_b = w_ple[0].astype(BF16)

    def trunk(x3, pe3, pos, shift_prev, wkv_prev, chunk, head_group, attend):
        b, t, _ = x3.shape
        n = b * t
        x = x3.reshape(n, d)
        tl = _tiles(n, t, page_table.shape[1])
        tm = tl["in_proj_rows"]
        cos64, sin64 = _rope_tables(pos)
        pad = IN_BLK - QK_ROPE
        c_misc = _tile_rows(jnp.concatenate([jnp.ones_like(cos64), cos64], axis=1), tm)
        s_misc = _tile_rows(jnp.concatenate([jnp.zeros_like(sin64), sin64], axis=1), tm)
        zp = _in_proj(x, w_perm, gains, c_misc, s_misc, d_model=d, tm=tm)
        ckv = zp[:, ckv_block * IN_BLK:(ckv_block + 1) * IN_BLK]
        kpe = zp[:, misc_block * IN_BLK + pad:]
        tmq = tl["q_proj_rows"]
        c_q = _tile_rows(jnp.concatenate([cos64, cos64], axis=1), tmq)
        s_q = _tile_rows(jnp.concatenate([sin64, sin64], axis=1), tmq)
        q = _q_proj(zp, cq_block, w_uq_perm, c_q, s_q, tm=tmq)
        o_mla = attend(zp, q, ckv, kpe, tl)
        shift_perm = jnp.concatenate(
            [rw_perm(shift_prev), jnp.zeros((b, QK_ROPE), F32)], axis=-1)[:, None, :]
        o_rw, wkv_new = _rwkv(zp, misc_block, shift_perm, wkv_prev, vecs, w2p, a2p, g2,
                              batch=b, seq=t, chunk=chunk, head_group=head_group)
        h1 = _out_proj(zp, o_mla, o_rw, x, w_o_b, ln1_g, ln1_b, alpha=alpha, tm=tl["out_proj_rows"])
        h2 = _ffn(h1, wg_b, wu_b, wd_b, ln2_g, ln2_b, alpha=alpha, tm=tl["ffn_rows"], tf=tl["ffn_cols"])
        out = _ple(h2, w_pg_b, pe3.reshape(n, -1), w_pe_b, tm=tl["ple_rows"], tn=tl["ple_cols"])
        last = zp.reshape(b, t, -1)[:, -1]
        rw_last = jnp.concatenate([last[:, :3 * d], last[:, misc_block * IN_BLK:misc_block * IN_BLK + pad]], -1)
        shift_new = jnp.concatenate([
            rw_last[:, 0:d], rw_last[:, 3 * d + GATE_LORA:3 * d + GATE_LORA + DECAY_LORA],
            rw_last[:, d:3 * d], rw_last[:, 3 * d + GATE_LORA + DECAY_LORA:],
            rw_last[:, 3 * d:3 * d + GATE_LORA]], axis=-1)
        return (out.reshape(b, t, d), ckv.reshape(b, t, kv_lora), kpe.reshape(b, t, QK_ROPE),
                shift_new, wkv_new)

    def attend_prompt(zp, q, ckv, kpe, tl):
        kv = _mm(zp, ckv_block, kv_lora, w_kv, tm=tl["kv_proj_rows"], tn=tl["kv_proj_cols"],
                 out_dtype=BF16, name="kv_proj")
        return _attn_prompt(q, kv, kpe, batch=bp, seq=tp, tq=tl["attn_q_rows"])

    n_phys = cache_ckv.shape[1]
    cache_c = cache_ckv.reshape(n_phys, page, kv_lora)
    cache_r = jnp.swapaxes(cache_kpe.reshape(n_phys, page, QK_ROPE), 1, 2)

    def attend_sample(zp, q, ckv, kpe, tl):
        nope_cols = MLA_HEADS * QK_NOPE
        q_lat = _head_mm(q[:, :nope_cols], w_kv, MLA_HEADS, QK_NOPE, 0, contract_w_cols=True,
                         seq_out=(bs, ts), out_dtype=BF16, name="q_lat")
        q_pe = jnp.swapaxes(q[:, nope_cols:].reshape(bs, ts, MLA_HEADS, QK_ROPE), 1, 2)
        q_pe = q_pe.reshape(bs, MLA_HEADS * ts, QK_ROPE)
        o_lat = _attn_paged(page_table, q_lat, q_pe, ckv.reshape(bs, ts, kv_lora),
                            kpe.reshape(bs, ts, QK_ROPE), cache_c, cache_r,
                            pages=tl["pages_per_step"])
        return _head_mm(o_lat, w_kv, MLA_HEADS, V_HEAD, MLA_HEADS, contract_w_cols=False,
                        seq_out=None, out_dtype=F32, name="o_lat")

    pos_p = jnp.arange(tp, dtype=F32)
    pos_s = past_len + jnp.arange(ts, dtype=F32)
    shift0 = jnp.zeros((bp, rw_cols), F32)
    wkv0 = jnp.zeros((bp, d // RW_HEAD, RW_HEAD, RW_HEAD), F32)
    hp, c1, k1, s1, w1 = trunk(x_prompt, p_prompt[0], pos_p, shift0, wkv0, min(PROMPT_CHUNK, tp),
                               RW_GROUP_PROMPT, attend_prompt)
    hs, c2, k2, s2, w2 = trunk(x_sample, p_sample[0], pos_s, state_shift[0], state_wkv[0], ts,
                               RW_GROUP_SAMPLE, attend_sample)
    return (hp, hs, c1[None], k1[None], w1[None], s1[None], c2[None], k2[None], w2[None], s2[None])
```

```python
import functools

import jax
import jax.numpy as jnp
from jax import lax
from jax.experimental import pallas as pl
from jax.experimental.pallas import tpu as pltpu

F32 = jnp.float32
BF16 = jnp.bfloat16

MLA_HEADS = 16
QK_NOPE = 128
QK_ROPE = 64
V_HEAD = 128
RW_HEAD = 64
DECAY_LORA = 96
AAA_LORA = 96
GATE_LORA = 256
ROPE_THETA = 10000.0
SM_SCALE = (QK_NOPE + QK_ROPE) ** -0.5
GN_EPS = 64e-5
LN_EPS = 1e-5
RMS_EPS = 1e-6
DECAY_SCALE = 0.6065306597126334
PROMPT_CHUNK = 64
RW_GROUP_PROMPT = 32
RW_GROUP_SAMPLE = 32

LANES = 128
SUBLANES = 8
SEG_BLK = 256
VMEM_LIMIT_BYTES = 56 * 1024 * 1024

NEG_INF = float("-inf")
KK_NORM_FLOOR = 1e-12


def _tiles(n_rows, seq, n_pages):
    return dict(
        in_proj_rows=min(1024, n_rows),
        q_proj_rows=min(1024, n_rows),
        kv_proj_rows=min(1024, n_rows), kv_proj_cols=2048,
        out_proj_rows=min(256, n_rows),
        ffn_rows=min(512, n_rows), ffn_cols=512,
        ple_rows=min(1024, n_rows), ple_cols=1024,
        attn_q_rows=min(512, seq),
        pages_per_step=min(32, n_pages),
    )


def _cparams(sem, flags=None):
    return pltpu.CompilerParams(dimension_semantics=sem, vmem_limit_bytes=VMEM_LIMIT_BYTES, flags=flags)


def _dot(a, b):
    return jnp.dot(a.astype(BF16), b.astype(BF16), preferred_element_type=F32)


def _dot_nt(a, b):
    return lax.dot_general(a.astype(BF16), b.astype(BF16), (((1,), (1,)), ((), ())),
                           preferred_element_type=F32)


def _dot_tn(a, b):
    return lax.dot_general(a.astype(BF16), b.astype(BF16), (((0,), (0,)), ((), ())),
                           preferred_element_type=F32)


def _sigmoid(x):
    return 0.5 * jnp.tanh(0.5 * x) + 0.5


def _swap_halves(x, half):
    n = x.shape[-1]
    lane = lax.broadcasted_iota(jnp.int32, x.shape, x.ndim - 1)
    first = (lane % (2 * half)) < half
    return jnp.where(first, pltpu.roll(x, n - half, x.ndim - 1), pltpu.roll(x, half, x.ndim - 1))


IN_BLK = 512


def _in_proj_kernel(x_ref, w_ref, g_ref, c_ref, s_ref, o_ref, xb_ref, *, n_raw, n_sig):
    j = pl.program_id(1)

    @pl.when(j == 0)
    def _():
        xb_ref[...] = x_ref[...].astype(BF16)

    z = jnp.dot(xb_ref[...], w_ref[...], preferred_element_type=F32)

    @pl.when(j < n_raw)
    def _():
        o_ref[...] = z

    @pl.when((j >= n_raw) & (j < n_raw + n_sig))
    def _():
        o_ref[...] = _sigmoid(z)

    @pl.when((j >= n_raw + n_sig) & (j < n_raw + n_sig + 2))
    def _():
        ms = jnp.mean(z * z, axis=-1, keepdims=True)
        o_ref[...] = z * lax.rsqrt(ms + RMS_EPS) * g_ref[...]

    @pl.when(j == n_raw + n_sig + 2)
    def _():
        keep = IN_BLK - LANES
        tail = z[:, keep:]
        o_ref[:, :keep] = z[:, :keep]
        o_ref[:, keep:] = tail * c_ref[...] + _swap_halves(tail, QK_ROPE // 2) * s_ref[...]


def _in_proj(x, w_perm, gains, ctab, stab, *, d_model, tm):
    n = x.shape[0]
    cols = w_perm.shape[1]
    nj = cols // IN_BLK
    n_raw = 3 * d_model // IN_BLK
    n_sig = 2 * d_model // IN_BLK
    assert nj == n_raw + n_sig + 3
    ntab = ctab.shape[0] // tm
    return pl.pallas_call(
        functools.partial(_in_proj_kernel, n_raw=n_raw, n_sig=n_sig),
        grid=(n // tm, nj),
        in_specs=[
            pl.BlockSpec((tm, d_model), lambda i, j: (i, 0)),
            pl.BlockSpec((d_model, IN_BLK), lambda i, j: (0, j)),
            pl.BlockSpec((1, IN_BLK), lambda i, j: (0, j)),
            pl.BlockSpec((tm, LANES), lambda i, j: (i % ntab, 0)),
            pl.BlockSpec((tm, LANES), lambda i, j: (i % ntab, 0)),
        ],
        out_specs=pl.BlockSpec((tm, IN_BLK), lambda i, j: (i, j)),
        out_shape=jax.ShapeDtypeStruct((n, cols), F32),
        scratch_shapes=[pltpu.VMEM((tm, d_model), BF16)],
        compiler_params=_cparams(("arbitrary", "arbitrary")),
        name="in_proj",
    )(x, w_perm, gains, ctab, stab)


def _q_proj_kernel(cq_ref, w_ref, c_ref, s_ref, o_ref, *, n_nope_blocks):
    j = pl.program_id(1)
    z = _dot(cq_ref[...], w_ref[...])

    @pl.when(j < n_nope_blocks)
    def _():
        o_ref[...] = z.astype(o_ref.dtype)

    @pl.when(j >= n_nope_blocks)
    def _():
        reps = z.shape[1] // LANES
        c = jnp.concatenate([c_ref[...]] * reps, axis=1)
        s = jnp.concatenate([s_ref[...]] * reps, axis=1)
        o_ref[...] = (z * c + _swap_halves(z, QK_ROPE // 2) * s).astype(o_ref.dtype)


def _q_proj(zp, cq_block, w_uq_perm, ctab, stab, *, tm):
    n = zp.shape[0]
    q_lora = w_uq_perm.shape[0]
    cols = w_uq_perm.shape[1]
    tn = MLA_HEADS * QK_ROPE
    ntab = ctab.shape[0] // tm
    return pl.pallas_call(
        functools.partial(_q_proj_kernel, n_nope_blocks=MLA_HEADS * QK_NOPE // tn),
        grid=(n // tm, cols // tn),
        in_specs=[
            pl.BlockSpec((tm, q_lora), lambda i, j: (i, cq_block)),
            pl.BlockSpec((q_lora, tn), lambda i, j: (0, j)),
            pl.BlockSpec((tm, LANES), lambda i, j: (i % ntab, 0)),
            pl.BlockSpec((tm, LANES), lambda i, j: (i % ntab, 0)),
        ],
        out_specs=pl.BlockSpec((tm, tn), lambda i, j: (i, j)),
        out_shape=jax.ShapeDtypeStruct((n, cols), BF16),
        compiler_params=_cparams(("arbitrary", "arbitrary")),
        name="q_proj",
    )(zp, w_uq_perm, ctab, stab)


def _mm_kernel(x_ref, w_ref, o_ref):
    o_ref[...] = _dot(x_ref[...], w_ref[...]).astype(o_ref.dtype)


def _mm(x, x_block, k, w, *, tm, tn, out_dtype, name):
    n = x.shape[0]
    cols = w.shape[1]
    return pl.pallas_call(
        _mm_kernel,
        grid=(n // tm, cols // tn),
        in_specs=[
            pl.BlockSpec((tm, k), lambda i, j: (i, x_block)),
            pl.BlockSpec((k, tn), lambda i, j: (0, j)),
        ],
        out_specs=pl.BlockSpec((tm, tn), lambda i, j: (i, j)),
        out_shape=jax.ShapeDtypeStruct((n, cols), out_dtype),
        compiler_params=_cparams(("arbitrary", "arbitrary")),
        name=name,
    )(x, w)


def _head_mm_kernel(x_ref, w_ref, o_ref, *, contract_w_cols):
    dot = _dot_nt if contract_w_cols else _dot
    x = x_ref[...]
    if x.ndim == 3:
        x = x.astype(F32).reshape(x.shape[0] * x.shape[1], x.shape[2])
    res = dot(x, w_ref[...])
    o_ref[...] = res.reshape(o_ref.shape).astype(o_ref.dtype)


def _head_mm(x, w, heads, c, w_block0, *, contract_w_cols, seq_out, out_dtype, name):
    r = w.shape[0]
    m = r if contract_w_cols else c
    if x.ndim == 3:
        nseq, rows, k = x.shape
        t = rows // heads
        n = nseq * t
        x_spec = pl.BlockSpec((nseq, t, k), lambda i: (0, i, 0))
    else:
        n = x.shape[0]
        x_spec = pl.BlockSpec((n, x.shape[1] // heads), lambda i: (0, i))
    if seq_out is None:
        out_spec = pl.BlockSpec((n, m), lambda i: (0, i))
        out_shape = jax.ShapeDtypeStruct((n, heads * m), out_dtype)
    else:
        nseq, t = seq_out
        out_spec = pl.BlockSpec((nseq, t, m), lambda i: (0, i, 0))
        out_shape = jax.ShapeDtypeStruct((nseq, heads * t, m), out_dtype)
    return pl.pallas_call(
        functools.partial(_head_mm_kernel, contract_w_cols=contract_w_cols),
        grid=(heads,),
        in_specs=[x_spec, pl.BlockSpec((r, c), lambda i: (0, w_block0 + i))],
        out_specs=out_spec,
        out_shape=out_shape,
        compiler_params=_cparams(("arbitrary",)),
        name=name,
    )(x, w)


HEADS_PER_STEP = 8
ROW_SPLIT = 2


def _softmax_update(s, m_prev, l_prev):
    m_next = jnp.maximum(m_prev, jnp.max(s, axis=-1, keepdims=True))
    alpha = jnp.exp(m_prev - m_next)
    p = jnp.exp(s - m_next)
    l_next = alpha * l_prev + jnp.sum(p, axis=-1, keepdims=True)
    return m_next, l_next, alpha, p.astype(BF16)


def _attn_prompt_kernel(qi_ref, ki_ref, qn_ref, qp_ref, kn_ref, v_ref, kpe_ref, o_ref,
                        m_ref, l_ref, acc_ref, *, tq, tk):
    step = pl.program_id(2)
    qi = qi_ref[step]
    ki = ki_ref[step]
    heads = range(HEADS_PER_STEP)

    @pl.when(ki == 0)
    def _():
        m_ref[...] = jnp.full(m_ref.shape, NEG_INF, F32)
        l_ref[...] = jnp.zeros(l_ref.shape, F32)
        acc_ref[...] = jnp.zeros(acc_ref.shape, F32)

    def block(diagonal):
        kpe = kpe_ref[...].astype(BF16)
        rq = tq // ROW_SPLIT
        units = [(h, r * rq) for h in heads for r in range(ROW_SPLIT)]
        s = [(_dot_nt(qn_ref[r0:r0 + rq, h * QK_NOPE:(h + 1) * QK_NOPE],
                      kn_ref[:, h * QK_NOPE:(h + 1) * QK_NOPE])
              + _dot_nt(qp_ref[r0:r0 + rq, h * QK_ROPE:(h + 1) * QK_ROPE], kpe)) * SM_SCALE
             for h, r0 in units]
        if diagonal:
            s = [jnp.where(lax.broadcasted_iota(jnp.int32, (rq, tk), 1)
                           <= lax.broadcasted_iota(jnp.int32, (rq, tk), 0) + r0, su, NEG_INF)
                 for su, (h, r0) in zip(s, units)]
        upd = [_softmax_update(su, m_ref[h, r0:r0 + rq], l_ref[h, r0:r0 + rq])
               for su, (h, r0) in zip(s, units)]
        for (m_next, l_next, alpha, p), (h, r0) in zip(upd, units):
            m_ref[h, r0:r0 + rq] = m_next
            l_ref[h, r0:r0 + rq] = l_next
            acc_ref[h, r0:r0 + rq] = (alpha * acc_ref[h, r0:r0 + rq]
                                      + _dot(p, v_ref[:, h * V_HEAD:(h + 1) * V_HEAD]))

    @pl.when(ki < qi)
    def _():
        block(False)

    @pl.when(ki == qi)
    def _():
        block(True)
        for h in heads:
            o_ref[:, h * V_HEAD:(h + 1) * V_HEAD] = acc_ref[h] / l_ref[h]


def _attn_prompt(q, kv, kpe, *, batch, seq, tq):
    tk = tq
    nq = seq // tq
    qi_tab = jnp.asarray([qi for qi in range(nq) for _ in range(qi + 1)], jnp.int32)
    ki_tab = jnp.asarray([ki for qi in range(nq) for ki in range(qi + 1)], jnp.int32)
    n_pairs = MLA_HEADS // HEADS_PER_STEP
    nope_w = HEADS_PER_STEP * QK_NOPE
    pe_w = HEADS_PER_STEP * QK_ROPE
    v_w = HEADS_PER_STEP * V_HEAD
    pe_off = MLA_HEADS * QK_NOPE // pe_w
    v_off = MLA_HEADS * QK_NOPE // v_w
    grid_spec = pltpu.PrefetchScalarGridSpec(
        num_scalar_prefetch=2,
        grid=(batch, n_pairs, int(qi_tab.shape[0])),
        in_specs=[
            pl.BlockSpec((tq, nope_w), lambda b, hp, s, qt, kt: (b * nq + qt[s], hp)),
            pl.BlockSpec((tq, pe_w), lambda b, hp, s, qt, kt: (b * nq + qt[s], pe_off + hp)),
            pl.BlockSpec((tk, nope_w), lambda b, hp, s, qt, kt: (b * nq + kt[s], hp)),
            pl.BlockSpec((tk, v_w), lambda b, hp, s, qt, kt: (b * nq + kt[s], v_off + hp)),
            pl.BlockSpec((tk, QK_ROPE), lambda b, hp, s, qt, kt: (b * nq + kt[s], 0)),
        ],
        out_specs=pl.BlockSpec((tq, v_w), lambda b, hp, s, qt, kt: (b * nq + qt[s], hp)),
        scratch_shapes=[
            pltpu.VMEM((HEADS_PER_STEP, tq, 1), F32),
            pltpu.VMEM((HEADS_PER_STEP, tq, 1), F32),
            pltpu.VMEM((HEADS_PER_STEP, tq, V_HEAD), F32),
        ],
    )
    return pl.pallas_call(
        functools.partial(_attn_prompt_kernel, tq=tq, tk=tk),
        grid_spec=grid_spec,
        out_shape=jax.ShapeDtypeStruct((batch * seq, MLA_HEADS * V_HEAD), F32),
        compiler_params=_cparams(("arbitrary", "arbitrary", "arbitrary")),
        name="attn_prompt",
    )(qi_tab, ki_tab, q, q, kv, kv, kpe)


PAGE_SLOTS = 4


def _attn_paged_kernel(pt_ref, ql_ref, qp_ref, cn_ref, kn_ref, ckv_hbm, kpet_hbm, o_ref,
                       kbuf, pbuf, sem, m_ref, l_ref, acc_ref, *, pages, page, t_new):
    b = pl.program_id(0)
    g = pl.program_id(1)
    nb = pl.num_programs(0)
    ng = pl.num_programs(1)
    lin = b * ng + g

    groups = [(b, g, lax.rem(lin, PAGE_SLOTS))]
    for _ in range(PAGE_SLOTS - 1):
        pb, pg, ps = groups[-1]
        wrap = pg + 1 == ng
        groups.append((jnp.where(wrap, jnp.where(pb + 1 == nb, 0, pb + 1), pb),
                       jnp.where(wrap, 0, pg + 1),
                       jnp.where(ps + 1 == PAGE_SLOTS, 0, ps + 1)))

    def copies(ahead):
        bb, gg, sl = groups[ahead]
        out = []
        for i in range(pages):
            pid = pt_ref[bb, gg * pages + i]
            out.append(pltpu.make_async_copy(ckv_hbm.at[pid], kbuf.at[sl, i], sem.at[0, sl]))
            out.append(pltpu.make_async_copy(kpet_hbm.at[pid], pbuf.at[sl, i], sem.at[1, sl]))
        return out

    @pl.when(lin == 0)
    def _():
        for n in range(PAGE_SLOTS - 1):
            for k, c in enumerate(copies(n)):
                c.start(priority=(k // 2) % 2)

    @pl.when(g == 0)
    def _():
        m_ref[...] = jnp.full(m_ref.shape, NEG_INF, F32)
        l_ref[...] = jnp.zeros(l_ref.shape, F32)
        acc_ref[...] = jnp.zeros(acc_ref.shape, F32)

    for c in copies(0):
        c.wait()

    slot = groups[0][2]
    ahead = copies(PAGE_SLOTS - 1)
    ql = ql_ref[0]
    qp = qp_ref[0]
    keys, s = [], []
    for i in range(pages):
        ahead[2 * i].start(priority=i % 2)
        ahead[2 * i + 1].start(priority=i % 2)
        keys.append(kbuf[slot, i].astype(BF16))
        s.append(_dot_nt(ql, keys[i]) + _dot(qp, pbuf[slot, i]))
    half = pages // 2
    m_run, l_run, acc = m_ref[...], l_ref[...], acc_ref[...]
    for lo, hi in ((0, half), (half, pages)):
        s_part = jnp.concatenate(s[lo:hi], axis=-1) * SM_SCALE
        m_run, l_run, alpha, p = _softmax_update(s_part, m_run, l_run)
        pv = _dot(p[:, 0:page], keys[lo])
        for i in range(lo + 1, hi):
            pv = pv + _dot(p[:, (i - lo) * page:(i - lo + 1) * page], keys[i])
        acc = alpha * acc + pv
    m_ref[...] = m_run
    l_ref[...] = l_run
    acc_ref[...] = acc

    @pl.when(lin == nb * ng - 1)
    def _():
        for n in range(1, PAGE_SLOTS):
            for c in copies(n):
                c.wait()

    @pl.when(g == ng - 1)
    def _():
        cn = cn_ref[0].astype(BF16)
        rows = ql.shape[0]
        sn = (_dot_nt(ql, cn) + _dot_nt(qp, kn_ref[0])) * SM_SCALE
        t_row = lax.broadcasted_iota(jnp.int32, (rows, t_new), 0) % t_new
        t_col = lax.broadcasted_iota(jnp.int32, (rows, t_new), 1)
        sn = jnp.where(t_col <= t_row, sn, NEG_INF)
        m_prev = m_ref[...]
        m_next = jnp.maximum(m_prev, jnp.max(sn, axis=-1, keepdims=True))
        alpha = jnp.exp(m_prev - m_next)
        p = jnp.exp(sn - m_next)
        l = alpha * l_ref[...] + jnp.sum(p, axis=-1, keepdims=True)
        acc = alpha * acc_ref[...] + _dot(p, cn)
        o_ref[0] = (acc / l).astype(o_ref.dtype)


def _attn_paged(page_table, q_lat, q_pe, ckv_new, kpe_new, cache_ckv, cache_kpe_t, *, pages):
    nseq, rows, kv_lora = q_lat.shape
    t_new = ckv_new.shape[1]
    page = cache_ckv.shape[1]
    n_pages = page_table.shape[1]
    assert n_pages % pages == 0 and pages % 2 == 0
    in_specs = [
        pl.BlockSpec((1, rows, kv_lora), lambda b, g, pt: (b, 0, 0)),
        pl.BlockSpec((1, rows, QK_ROPE), lambda b, g, pt: (b, 0, 0)),
        pl.BlockSpec((1, t_new, kv_lora), lambda b, g, pt: (b, 0, 0)),
        pl.BlockSpec((1, t_new, QK_ROPE), lambda b, g, pt: (b, 0, 0)),
        pl.BlockSpec(memory_space=pl.ANY),
        pl.BlockSpec(memory_space=pl.ANY),
    ]
    grid_spec = pltpu.PrefetchScalarGridSpec(
        num_scalar_prefetch=1,
        grid=(nseq, n_pages // pages),
        in_specs=in_specs,
        out_specs=pl.BlockSpec((1, rows, kv_lora), lambda b, g, pt: (b, 0, 0)),
        scratch_shapes=[
            pltpu.VMEM((PAGE_SLOTS, pages, page, kv_lora), F32),
            pltpu.VMEM((PAGE_SLOTS, pages, QK_ROPE, page), F32),
            pltpu.SemaphoreType.DMA((2, PAGE_SLOTS)),
            pltpu.VMEM((rows, 1), F32),
            pltpu.VMEM((rows, 1), F32),
            pltpu.VMEM((rows, kv_lora), F32),
        ],
    )
    return pl.pallas_call(
        functools.partial(_attn_paged_kernel, pages=pages, page=page, t_new=t_new),
        grid_spec=grid_spec,
        out_shape=jax.ShapeDtypeStruct((nseq, rows, kv_lora), BF16),
        compiler_params=_cparams(("arbitrary", "arbitrary")),
        name="attn_paged",
    )(page_table, q_lat, q_pe, ckv_new, kpe_new, cache_ckv, cache_kpe_t)


def _rwkv_kernel(r_ref, k_ref, v_ref, misc_ref, shift_ref, s0_ref,
                 mu_ref, w0_ref, a0_ref, kkw_ref, kaw_ref, rk_ref, gng_ref, gnb_ref,
                 w2_ref, a2_ref, g2_ref,
                 o_ref, sout_ref,
                 prev_ref, st_ref, wl_ref, rt_ref, kt_ref, vv_ref, at_ref, bt_ref, bon_ref, gate_ref,
                 y_ref, *, chunk, heads, head_group):
    c = pl.program_id(1)
    nc = pl.num_programs(1)
    d = r_ref.shape[1]
    misc_w = misc_ref.shape[1]
    lora_w = w2_ref.shape[0]

    @pl.when(c == 0)
    def _():
        prev_ref[...] = shift_ref[0]
        st_ref[...] = s0_ref[0]

    sub = SUBLANES
    row0 = lax.broadcasted_iota(jnp.int32, (sub, 1), 0) == 0

    def mix(z, lo, width):
        rolled = pltpu.roll(z, 1, 0)
        first = jnp.where(row0, prev_ref[:, lo:lo + width], rolled[:sub])
        prev = first if chunk == sub else jnp.concatenate([first, rolled[sub:]], axis=0)
        return z + (prev - z) * mu_ref[:, lo:lo + width]

    r_raw = r_ref[...]
    k_raw = k_ref[...]
    v_raw = v_ref[...]
    m_raw = misc_ref[...]
    r = mix(r_raw, 0, d)
    k = mix(k_raw, d, d)
    v = mix(v_raw, 2 * d, d)
    m = mix(m_raw, 3 * d, misc_w)
    last = chunk - 1
    prev_ref[:, 0:d] = r_raw[last:last + 1]
    prev_ref[:, d:2 * d] = k_raw[last:last + 1]
    prev_ref[:, 2 * d:3 * d] = v_raw[last:last + 1]
    prev_ref[:, 3 * d:3 * d + misc_w] = m_raw[last:last + 1]

    g_lo = m[:, 0:GATE_LORA]
    lo = m[:, GATE_LORA:GATE_LORA + lora_w]
    w_pre = w0_ref[...] + _dot(jnp.tanh(lo), w2_ref[...])
    a = _sigmoid(a0_ref[...] + _dot(lo, a2_ref[...]))
    gate = _dot(_sigmoid(g_lo), g2_ref[...])
    lw = -DECAY_SCALE * _sigmoid(w_pre)
    ti = lax.broadcasted_iota(jnp.int32, (chunk, chunk), 0)
    tj = lax.broadcasted_iota(jnp.int32, (chunk, chunk), 1)
    incl = ti >= tj
    strict = ti > tj
    tri = incl.astype(BF16)
    hi = lw.astype(BF16)
    mid = (lw - hi.astype(F32)).astype(BF16)
    low = (lw - hi.astype(F32) - mid.astype(F32)).astype(BF16)
    cs = (jnp.dot(tri, hi, preferred_element_type=F32) + jnp.dot(tri, mid, preferred_element_type=F32)
          + jnp.dot(tri, low, preferred_element_type=F32))
    gi_ = lax.broadcasted_iota(jnp.int32, (SEG_BLK, SEG_BLK), 0) // RW_HEAD
    gj_ = lax.broadcasted_iota(jnp.int32, (SEG_BLK, SEG_BLK), 1) // RW_HEAD
    seg = (gi_ == gj_).astype(BF16)

    def head_sum(x):
        x_hi = x.astype(BF16).astype(F32)
        x_lo = x - x_hi
        cols = range(0, d, SEG_BLK)
        stacked = jnp.concatenate([x_hi[:, j:j + SEG_BLK] for j in cols]
                                  + [x_lo[:, j:j + SEG_BLK] for j in cols], axis=0)
        sums = jnp.dot(stacked.astype(BF16), seg, preferred_element_type=F32)
        nblk = len(cols)
        return jnp.concatenate(
            [sums[j * chunk:(j + 1) * chunk] + sums[(nblk + j) * chunk:(nblk + j + 1) * chunk]
             for j in range(nblk)], axis=1)

    w_inv = jnp.exp(-cs)
    kp = k * (1.0 + (a - 1.0) * kaw_ref[...])
    kkr = k * kkw_ref[...]
    kk = kkr * lax.rsqrt(jnp.maximum(head_sum(kkr * kkr), KK_NORM_FLOOR * KK_NORM_FLOOR))
    w_c = jnp.exp(cs)
    rt_ref[...] = r * w_c
    kt_ref[...] = kp * w_inv
    vv_ref[...] = v
    at_ref[...] = -kk * jnp.exp(cs - lw)
    bt_ref[...] = kk * a * w_inv
    wl_ref[...] = w_c[last:last + 1]
    bon_ref[...] = head_sum(r * kp * rk_ref[...])
    gate_ref[...] = gate

    n_double = max(1, (chunk - 1).bit_length())
    gw = head_group * RW_HEAD

    def group_body(gi, carry):
        off = pl.multiple_of(gi * gw, gw)
        sl = pl.ds(off, gw)
        rt_g = rt_ref[:, sl]
        kt_g = kt_ref[:, sl]
        v_g = vv_ref[:, sl]
        at_g = at_ref[:, sl]
        bt_g = bt_ref[:, sl]
        wl_g = wl_ref[:, sl]
        s0s = [st_ref[gi * head_group + i] for i in range(head_group)]
        hrange = range(head_group)
        hsl = [slice(i * RW_HEAD, (i + 1) * RW_HEAD) for i in hrange]
        vh = [v_g[:, hs] for hs in hsl]
        lhs = [jnp.concatenate([at_g[:, hs], rt_g[:, hs]], axis=0) for hs in hsl]
        rhs = [jnp.concatenate([bt_g[:, hs], kt_g[:, hs]], axis=0) for hs in hsl]
        gram = [_dot_nt(lhs[i], rhs[i]) for i in hrange]
        h0 = [_dot_nt(lhs[i], s0s[i]) for i in hrange]
        pw = [jnp.where(strict, gram[i][:chunk, :chunk], 0.0) for i in hrange]
        u = [h0[i][:chunk] + _dot(jnp.where(strict, gram[i][:chunk, chunk:], 0.0), vh[i]) for i in hrange]
        for step in range(n_double):
            u = [u[i] + _dot(pw[i], u[i]) for i in hrange]
            if step + 1 < n_double:
                pw = [_dot(pw[i], pw[i]) for i in hrange]
        uv = [jnp.concatenate([u[i], vh[i]], axis=0) for i in hrange]
        ys = []
        for i in hrange:
            a_r = jnp.concatenate([jnp.where(incl, gram[i][chunk:, :chunk], 0.0),
                                   jnp.where(incl, gram[i][chunk:, chunk:], 0.0)], axis=1)
            ys.append(h0[i][chunk:] + _dot(a_r, uv[i]))
        s_new = [(s0s[i] + _dot_tn(uv[i], rhs[i])) * wl_g[:, hsl[i]] for i in hrange]
        y_ref[:, sl] = jnp.concatenate(ys, axis=-1)
        for i in range(head_group):
            st_ref[gi * head_group + i] = s_new[i]
        return carry

    lax.fori_loop(0, heads // head_group, group_body, 0)
    y = y_ref[...]
    yc = y - head_sum(y) * (1.0 / RW_HEAD)
    var = head_sum(yc * yc) * (1.0 / RW_HEAD)
    yn = yc * lax.rsqrt(var + GN_EPS) * gng_ref[...] + gnb_ref[...]
    o_ref[...] = (yn + bon_ref[...] * vv_ref[...]) * gate_ref[...]

    @pl.when(c == nc - 1)
    def _():
        sout_ref[0] = st_ref[...]


def _rwkv(zp, misc_block, shift_perm, s0, vecs, w2p, a2p, g2, *, batch, seq, chunk, head_group):
    d = g2.shape[1]
    heads = d // RW_HEAD
    nc = seq // chunk
    misc_w = IN_BLK
    row = lambda b, c: b * nc + c
    full = lambda shape: pl.BlockSpec(shape, lambda b, c: (0,) * len(shape))
    mu, w0, a0, kkw, kaw, rk, gng, gnb = vecs
    big = pltpu.VMEM((chunk, d), F32)
    return pl.pallas_call(
        functools.partial(_rwkv_kernel, chunk=chunk, heads=heads, head_group=head_group),
        grid=(batch, nc),
        in_specs=[
            pl.BlockSpec((chunk, d), lambda b, c: (row(b, c), 0)),
            pl.BlockSpec((chunk, d), lambda b, c: (row(b, c), 1)),
            pl.BlockSpec((chunk, d), lambda b, c: (row(b, c), 2)),
            pl.BlockSpec((chunk, misc_w), lambda b, c: (row(b, c), misc_block)),
            pl.BlockSpec((1, 1, 3 * d + misc_w), lambda b, c: (b, 0, 0)),
            pl.BlockSpec((1, heads, RW_HEAD, RW_HEAD), lambda b, c: (b, 0, 0, 0)),
            full((1, 3 * d + misc_w)),
            full((1, d)), full((1, d)), full((1, d)), full((1, d)), full((1, d)), full((1, d)), full((1, d)),
            full(w2p.shape), full(a2p.shape), full(g2.shape),
        ],
        out_specs=[
            pl.BlockSpec((chunk, d), lambda b, c: (row(b, c), 0)),
            pl.BlockSpec((1, heads, RW_HEAD, RW_HEAD), lambda b, c: (b, 0, 0, 0)),
        ],
        out_shape=[
            jax.ShapeDtypeStruct((batch * seq, d), F32),
            jax.ShapeDtypeStruct((batch, heads, RW_HEAD, RW_HEAD), F32),
        ],
        scratch_shapes=[
            pltpu.VMEM((1, 3 * d + misc_w), F32),
            pltpu.VMEM((heads, RW_HEAD, RW_HEAD), F32),
            pltpu.VMEM((1, d), F32),
            big, big, big, big, big, big, big, big,
        ],
        compiler_params=_cparams(("arbitrary", "arbitrary")),
        name="rwkv",
    )(zp, zp, zp, zp, shift_perm, s0, mu, w0, a0, kkw, kaw, rk, gng, gnb, w2p, a2p, g2)


def _layer_norm(x, g, b):
    mu = jnp.mean(x, axis=-1, keepdims=True)
    xc = x - mu
    var = jnp.mean(xc * xc, axis=-1, keepdims=True)
    return xc * lax.rsqrt(var + LN_EPS) * g + b


def _out_proj_kernel(gm_ref, gr_ref, om_ref, or_ref, x_ref, w_ref, g_ref, b_ref, o_ref, *, alpha):
    merged = gm_ref[...] * om_ref[...] + gr_ref[...] * or_ref[...]
    y = alpha * x_ref[...] + _dot(merged, w_ref[...])
    o_ref[...] = _layer_norm(y, g_ref[...], b_ref[...])


def _out_proj(zp, o_mla, o_rw, x, w_o, ln_g, ln_b, *, alpha, tm):
    n, d = x.shape
    rows = lambda blk: pl.BlockSpec((tm, d), lambda i: (i, blk))
    const = lambda shape: pl.BlockSpec(shape, lambda i: (0, 0))
    return pl.pallas_call(
        functools.partial(_out_proj_kernel, alpha=alpha),
        grid=(n // tm,),
        in_specs=[rows(3), rows(4), rows(0), rows(0), rows(0), const((d, d)), const((1, d)), const((1, d))],
        out_specs=rows(0),
        out_shape=jax.ShapeDtypeStruct((n, d), F32),
        compiler_params=_cparams(("arbitrary",)),
        name="out_proj",
    )(zp, zp, o_mla, o_rw, x, w_o, ln_g, ln_b)


def _ffn_kernel(h_ref, wg_ref, wu_ref, wd_ref, g_ref, b_ref, o_ref, hb_ref, acc_ref, *, alpha):
    j = pl.program_id(1)

    @pl.when(j == 0)
    def _():
        hb_ref[...] = h_ref[...].astype(BF16)
        acc_ref[...] = jnp.zeros(acc_ref.shape, F32)

    hb = hb_ref[...]
    gate = jnp.dot(hb, wg_ref[...], preferred_element_type=F32)
    up = jnp.dot(hb, wu_ref[...], preferred_element_type=F32)
    acc_ref[...] += _dot(gate * _sigmoid(gate) * up, wd_ref[...])

    @pl.when(j == pl.num_programs(1) - 1)
    def _():
        o_ref[...] = _layer_norm(alpha * h_ref[...] + acc_ref[...], g_ref[...], b_ref[...])


def _ffn(h, wg, wu, wd, ln_g, ln_b, *, alpha, tm, tf):
    n, d = h.shape
    d_ff = wg.shape[1]
    return pl.pallas_call(
        functools.partial(_ffn_kernel, alpha=alpha),
        grid=(n // tm, d_ff // tf),
        in_specs=[
            pl.BlockSpec((tm, d), lambda i, j: (i, 0)),
            pl.BlockSpec((d, tf), lambda i, j: (0, j)),
            pl.BlockSpec((d, tf), lambda i, j: (0, j)),
            pl.BlockSpec((tf, d), lambda i, j: (j, 0)),
            pl.BlockSpec((1, d), lambda i, j: (0, 0)),
            pl.BlockSpec((1, d), lambda i, j: (0, 0)),
        ],
        out_specs=pl.BlockSpec((tm, d), lambda i, j: (i, 0)),
        out_shape=jax.ShapeDtypeStruct((n, d), F32),
        scratch_shapes=[pltpu.VMEM((tm, d), BF16), pltpu.VMEM((tm, d), F32)],
        compiler_params=_cparams(("arbitrary", "arbitrary")),
        name="ffn",
    )(h, wg, wu, wd, ln_g, ln_b)


def _ple_kernel(h_ref, wpg_ref, pe_ref, wpe_ref, o_ref, hb_ref, *, tn):
    j = pl.program_id(1)

    @pl.when(j == 0)
    def _():
        hb_ref[...] = h_ref[...].astype(BF16)

    gate = _sigmoid(jnp.dot(hb_ref[...], wpg_ref[...], preferred_element_type=F32))
    emb = _dot(pe_ref[...], wpe_ref[...])
    o_ref[...] = h_ref[:, pl.ds(pl.multiple_of(j * tn, tn), tn)] + gate * emb


def _ple(h, w_pg, pe, w_pe, *, tm, tn):
    n, d = h.shape
    p = pe.shape[1]
    return pl.pallas_call(
        functools.partial(_ple_kernel, tn=tn),
        grid=(n // tm, d // tn),
        in_specs=[
            pl.BlockSpec((tm, d), lambda i, j: (i, 0)),
            pl.BlockSpec((d, tn), lambda i, j: (0, j)),
            pl.BlockSpec((tm, p), lambda i, j: (i, 0)),
            pl.BlockSpec((p, tn), lambda i, j: (0, j)),
        ],
        out_specs=pl.BlockSpec((tm, tn), lambda i, j: (i, j)),
        out_shape=jax.ShapeDtypeStruct((n, d), F32),
        scratch_shapes=[pltpu.VMEM((tm, d), BF16)],
        compiler_params=_cparams(("arbitrary", "arbitrary")),
        name="ple",
    )(h, w_pg, pe, w_pe)


def _rope_tables(pos):
    half = QK_ROPE // 2
    inv = ROPE_THETA ** (-jnp.arange(half, dtype=F32) / half)
    ang = pos[:, None] * inv[None, :]
    cos = jnp.cos(ang)
    sin = jnp.sin(ang)
    return jnp.concatenate([cos, cos], axis=-1), jnp.concatenate([-sin, sin], axis=-1)


def _tile_rows(t, tm):
    reps = max(1, tm // t.shape[0])
    return jnp.tile(t, (reps, 1)) if reps > 1 else t


def kernel(x_prompt, x_sample, p_prompt, p_sample, cache_ckv, cache_kpe, state_wkv, state_shift, page_table, w_in, mu_shift, g_q, g_kv, w_uq, w_uk, w_uv, rw_w0, rw_w2, rw_a0, rw_a2, rw_g2, rw_kk, rw_ka, rw_rk, gn_g, gn_b, w_o, ln1_g, ln1_b, w_ffn_gate, w_ffn_up, w_ffn_down, ln2_g, ln2_b, w_ple, w_ple_gate):
    depth = w_in.shape[0]
    assert depth == 1, "single-layer trunk"
    bp, tp, d = x_prompt.shape
    bs, ts, _ = x_sample.shape
    q_lora = g_q.shape[1]
    kv_lora = g_kv.shape[1]
    page = cache_ckv.shape[2]
    past_len = page_table.shape[1] * page
    rw_cols = mu_shift.shape[1]
    mla_cols = q_lora + kv_lora + QK_ROPE
    off_gate = mla_cols + rw_cols
    alpha = (2.0 * depth) ** 0.25
    assert d // RW_HEAD * RW_HEAD == d and q_lora == IN_BLK and kv_lora == IN_BLK
    assert GATE_LORA + DECAY_LORA + AAA_LORA + QK_ROPE == IN_BLK and 2 * QK_ROPE == LANES
    assert QK_NOPE == V_HEAD

    o_r, o_wlo, o_k, o_v = 0, d, d + DECAY_LORA, 2 * d + DECAY_LORA
    o_alo = 3 * d + DECAY_LORA
    o_glo = o_alo + AAA_LORA

    def rw_perm(t):
        return jnp.concatenate([
            t[..., o_r:o_r + d], t[..., o_k:o_k + d], t[..., o_v:o_v + d],
            t[..., o_glo:o_glo + GATE_LORA], t[..., o_wlo:o_wlo + DECAY_LORA],
            t[..., o_alo:o_alo + AAA_LORA]], axis=-1)

    wi = w_in[0].astype(BF16)
    w_rw = rw_perm(wi[:, mla_cols:off_gate])
    w_perm = jnp.concatenate([
        w_rw[:, :3 * d], wi[:, off_gate:], wi[:, :q_lora + kv_lora],
        w_rw[:, 3 * d:], wi[:, q_lora + kv_lora:mla_cols]], axis=-1)
    cq_block = 5 * d // IN_BLK
    ckv_block = cq_block + 1
    misc_block = cq_block + 2
    ones = jnp.ones((1, IN_BLK), F32)
    gains = jnp.concatenate(
        [jnp.ones((1, 5 * d), F32), g_q[0][None], g_kv[0][None], ones], axis=-1)
    zeros_kpe = jnp.zeros((1, QK_ROPE), F32)
    mu_perm = jnp.concatenate([rw_perm(mu_shift[0])[None], zeros_kpe], axis=-1)

    wq = w_uq[0]
    w_uq_perm = jnp.concatenate([
        wq[:, :, :QK_NOPE].reshape(q_lora, MLA_HEADS * QK_NOPE),
        wq[:, :, QK_NOPE:].reshape(q_lora, MLA_HEADS * QK_ROPE)], axis=-1).astype(BF16)
    w_kv = jnp.concatenate([
        w_uk[0].reshape(kv_lora, MLA_HEADS * QK_NOPE),
        w_uv[0].reshape(kv_lora, MLA_HEADS * V_HEAD)], axis=-1).astype(BF16)

    lora_w = IN_BLK - GATE_LORA
    w2p = jnp.zeros((lora_w, d), F32).at[:DECAY_LORA].set(rw_w2[0]).astype(BF16)
    a2p = jnp.zeros((lora_w, d), F32).at[DECAY_LORA:DECAY_LORA + AAA_LORA].set(rw_a2[0]).astype(BF16)
    g2 = rw_g2[0].astype(BF16)
    vecs = (mu_perm, rw_w0, rw_a0, rw_kk, rw_ka, rw_rk[0].reshape(1, d), gn_g, gn_b)

    w_o_b = w_o[0].astype(BF16)
    wg_b = w_ffn_gate[0].astype(BF16)
    wu_b = w_ffn_up[0].astype(BF16)
    wd_b = w_ffn_down[0].astype(BF16)
    w_pg_b = w_ple_gate[0].astype(BF16)
    w_pe_b = w_ple[0].astype(BF16)

    def trunk(x3, pe3, pos, shift_prev, wkv_prev, chunk, head_group, attend):
        b, t, _ = x3.shape
        n = b * t
        x = x3.reshape(n, d)
        tl = _tiles(n, t, page_table.shape[1])
        tm = tl["in_proj_rows"]
        cos64, sin64 = _rope_tables(pos)
        pad = IN_BLK - QK_ROPE
        c_misc = _tile_rows(jnp.concatenate([jnp.ones_like(cos64), cos64], axis=1), tm)
        s_misc = _tile_rows(jnp.concatenate([jnp.zeros_like(sin64), sin64], axis=1), tm)
        zp = _in_proj(x, w_perm, gains, c_misc, s_misc, d_model=d, tm=tm)
        ckv = zp[:, ckv_block * IN_BLK:(ckv_block + 1) * IN_BLK]
        kpe = zp[:, misc_block * IN_BLK + pad:]
        tmq = tl["q_proj_rows"]
        c_q = _tile_rows(jnp.concatenate([cos64, cos64], axis=1), tmq)
        s_q = _tile_rows(jnp.concatenate([sin64, sin64], axis=1), tmq)
        q = _q_proj(zp, cq_block, w_uq_perm, c_q, s_q, tm=tmq)
        o_mla = attend(zp, q, ckv, kpe, tl)
        shift_perm = jnp.concatenate(
            [rw_perm(shift_prev), jnp.zeros((b, QK_ROPE), F32)], axis=-1)[:, None, :]
        o_rw, wkv_new = _rwkv(zp, misc_block, shift_perm, wkv_prev, vecs, w2p, a2p, g2,
                              batch=b, seq=t, chunk=chunk, head_group=head_group)
        h1 = _out_proj(zp, o_mla, o_rw, x, w_o_b, ln1_g, ln1_b, alpha=alpha, tm=tl["out_proj_rows"])
        h2 = _ffn(h1, wg_b, wu_b, wd_b, ln2_g, ln2_b, alpha=alpha, tm=tl["ffn_rows"], tf=tl["ffn_cols"])
        out = _ple(h2, w_pg_b, pe3.reshape(n, -1), w_pe_b, tm=tl["ple_rows"], tn=tl["ple_cols"])
        last = zp.reshape(b, t, -1)[:, -1]
        rw_last = jnp.concatenate([last[:, :3 * d], last[:, misc_block * IN_BLK:misc_block * IN_BLK + pad]], -1)
        shift_new = jnp.concatenate([
            rw_last[:, 0:d], rw_last[:, 3 * d + GATE_LORA:3 * d + GATE_LORA + DECAY_LORA],
            rw_last[:, d:3 * d], rw_last[:, 3 * d + GATE_LORA + DECAY_LORA:],
            rw_last[:, 3 * d:3 * d + GATE_LORA]], axis=-1)
        return (out.reshape(b, t, d), ckv.reshape(b, t, kv_lora), kpe.reshape(b, t, QK_ROPE),
                shift_new, wkv_new)

    def attend_prompt(zp, q, ckv, kpe, tl):
        kv = _mm(zp, ckv_block, kv_lora, w_kv, tm=tl["kv_proj_rows"], tn=tl["kv_proj_cols"],
                 out_dtype=BF16, name="kv_proj")
        return _attn_prompt(q, kv, kpe, batch=bp, seq=tp, tq=tl["attn_q_rows"])

    n_phys = cache_ckv.shape[1]
    cache_c = cache_ckv.reshape(n_phys, page, kv_lora)
    cache_r = jnp.swapaxes(cache_kpe.reshape(n_phys, page, QK_ROPE), 1, 2)

    def attend_sample(zp, q, ckv, kpe, tl):
        nope_cols = MLA_HEADS * QK_NOPE
        q_lat = _head_mm(q[:, :nope_cols], w_kv, MLA_HEADS, QK_NOPE, 0, contract_w_cols=True,
                         seq_out=(bs, ts), out_dtype=BF16, name="q_lat")
        q_pe = jnp.swapaxes(q[:, nope_cols:].reshape(bs, ts, MLA_HEADS, QK_ROPE), 1, 2)
        q_pe = q_pe.reshape(bs, MLA_HEADS * ts, QK_ROPE)
        o_lat = _attn_paged(page_table, q_lat, q_pe, ckv.reshape(bs, ts, kv_lora),
                            kpe.reshape(bs, ts, QK_ROPE), cache_c, cache_r,
                            pages=tl["pages_per_step"])
        return _head_mm(o_lat, w_kv, MLA_HEADS, V_HEAD, MLA_HEADS, contract_w_cols=False,
                        seq_out=None, out_dtype=F32, name="o_lat")

    pos_p = jnp.arange(tp, dtype=F32)
    pos_s = past_len + jnp.arange(ts, dtype=F32)
    shift0 = jnp.zeros((bp, rw_cols), F32)
    wkv0 = jnp.zeros((bp, d // RW_HEAD, RW_HEAD, RW_HEAD), F32)
    hp, c1, k1, s1, w1 = trunk(x_prompt, p_prompt[0], pos_p, shift0, wkv0, min(PROMPT_CHUNK, tp),
                               RW_GROUP_PROMPT, attend_prompt)
    hs, c2, k2, s2, w2 = trunk(x_sample, p_sample[0], pos_s, state_shift[0], state_wkv[0], ts,
                               RW_GROUP_SAMPLE, attend_sample)
    return (hp, hs, c1[None], k1[None], w1[None], s1[None], c2[None], k2[None], w2[None], s2[None])
```

```python
import functools

import jax
import jax.numpy as jnp
from jax import lax
from jax.experimental import pallas as pl
from jax.experimental.pallas import tpu as pltpu

F32 = jnp.float32
BF16 = jnp.bfloat16

MLA_HEADS = 16
QK_NOPE = 128
QK_ROPE = 64
V_HEAD = 128
RW_HEAD = 64
DECAY_LORA = 96
AAA_LORA = 96
GATE_LORA = 256
ROPE_THETA = 10000.0
SM_SCALE = (QK_NOPE + QK_ROPE) ** -0.5
GN_EPS = 64e-5
LN_EPS = 1e-5
RMS_EPS = 1e-6
DECAY_SCALE = 0.6065306597126334
PROMPT_CHUNK = 64
RW_GROUP_PROMPT = 32
RW_GROUP_SAMPLE = 32

LANES = 128
SUBLANES = 8
SEG_BLK = 256
VMEM_LIMIT_BYTES = 56 * 1024 * 1024

NEG_INF = float("-inf")
KK_NORM_FLOOR = 1e-12


def _tiles(n_rows, seq, n_pages):
    return dict(
        in_proj_rows=min(1024, n_rows),
        q_proj_rows=min(1024, n_rows),
        kv_proj_rows=min(1024, n_rows), kv_proj_cols=2048,
        out_proj_rows=min(256, n_rows),
        ffn_rows=min(512, n_rows), ffn_cols=512,
        ple_rows=min(1024, n_rows), ple_cols=1024,
        attn_q_rows=min(512, seq),
        pages_per_step=min(32, n_pages),
    )


def _cparams(sem, flags=None):
    return pltpu.CompilerParams(dimension_semantics=sem, vmem_limit_bytes=VMEM_LIMIT_BYTES, flags=flags)


def _dot(a, b):
    return jnp.dot(a.astype(BF16), b.astype(BF16), preferred_element_type=F32)


def _dot_nt(a, b):
    return lax.dot_general(a.astype(BF16), b.astype(BF16), (((1,), (1,)), ((), ())),
                           preferred_element_type=F32)


def _dot_tn(a, b):
    return lax.dot_general(a.astype(BF16), b.astype(BF16), (((0,), (0,)), ((), ())),
                           preferred_element_type=F32)


def _sigmoid(x):
    return 0.5 * jnp.tanh(0.5 * x) + 0.5


def _swap_halves(x, half):
    n = x.shape[-1]
    lane = lax.broadcasted_iota(jnp.int32, x.shape, x.ndim - 1)
    first = (lane % (2 * half)) < half
    return jnp.where(first, pltpu.roll(x, n - half, x.ndim - 1), pltpu.roll(x, half, x.ndim - 1))


IN_BLK = 512


def _in_proj_kernel(x_ref, w_ref, g_ref, c_ref, s_ref, o_ref, xb_ref, *, n_raw, n_sig):
    j = pl.program_id(1)

    @pl.when(j == 0)
    def _():
        xb_ref[...] = x_ref[...].astype(BF16)

    z = jnp.dot(xb_ref[...], w_ref[...], preferred_element_type=F32)

    @pl.when(j < n_raw)
    def _():
        o_ref[...] = z

    @pl.when((j >= n_raw) & (j < n_raw + n_sig))
    def _():
        o_ref[...] = _sigmoid(z)

    @pl.when((j >= n_raw + n_sig) & (j < n_raw + n_sig + 2))
    def _():
        ms = jnp.mean(z * z, axis=-1, keepdims=True)
        o_ref[...] = z * lax.rsqrt(ms + RMS_EPS) * g_ref[...]

    @pl.when(j == n_raw + n_sig + 2)
    def _():
        keep = IN_BLK - LANES
        tail = z[:, keep:]
        o_ref[:, :keep] = z[:, :keep]
        o_ref[:, keep:] = tail * c_ref[...] + _swap_halves(tail, QK_ROPE // 2) * s_ref[...]


def _in_proj(x, w_perm, gains, ctab, stab, *, d_model, tm):
    n = x.shape[0]
    cols = w_perm.shape[1]
    nj = cols // IN_BLK
    n_raw = 3 * d_model // IN_BLK
    n_sig = 2 * d_model // IN_BLK
    assert nj == n_raw + n_sig + 3
    ntab = ctab.shape[0] // tm
    return pl.pallas_call(
        functools.partial(_in_proj_kernel, n_raw=n_raw, n_sig=n_sig),
        grid=(n // tm, nj),
        in_specs=[
            pl.BlockSpec((tm, d_model), lambda i, j: (i, 0)),
            pl.BlockSpec((d_model, IN_BLK), lambda i, j: (0, j)),
            pl.BlockSpec((1, IN_BLK), lambda i, j: (0, j)),
            pl.BlockSpec((tm, LANES), lambda i, j: (i % ntab, 0)),
            pl.BlockSpec((tm, LANES), lambda i, j: (i % ntab, 0)),
        ],
        out_specs=pl.BlockSpec((tm, IN_BLK), lambda i, j: (i, j)),
        out_shape=jax.ShapeDtypeStruct((n, cols), F32),
        scratch_shapes=[pltpu.VMEM((tm, d_model), BF16)],
        compiler_params=_cparams(("arbitrary", "arbitrary")),
        name="in_proj",
    )(x, w_perm, gains, ctab, stab)


def _q_proj_kernel(cq_ref, w_ref, c_ref, s_ref, o_ref, *, n_nope_blocks):
    j = pl.program_id(1)
    z = _dot(cq_ref[...], w_ref[...])

    @pl.when(j < n_nope_blocks)
    def _():
        o_ref[...] = z.astype(o_ref.dtype)

    @pl.when(j >= n_nope_blocks)
    def _():
        reps = z.shape[1] // LANES
        c = jnp.concatenate([c_ref[...]] * reps, axis=1)
        s = jnp.concatenate([s_ref[...]] * reps, axis=1)
        o_ref[...] = (z * c + _swap_halves(z, QK_ROPE // 2) * s).astype(o_ref.dtype)


def _q_proj(zp, cq_block, w_uq_perm, ctab, stab, *, tm):
    n = zp.shape[0]
    q_lora = w_uq_perm.shape[0]
    cols = w_uq_perm.shape[1]
    tn = MLA_HEADS * QK_ROPE
    ntab = ctab.shape[0] // tm
    return pl.pallas_call(
        functools.partial(_q_proj_kernel, n_nope_blocks=MLA_HEADS * QK_NOPE // tn),
        grid=(n // tm, cols // tn),
        in_specs=[
            pl.BlockSpec((tm, q_lora), lambda i, j: (i, cq_block)),
            pl.BlockSpec((q_lora, tn), lambda i, j: (0, j)),
            pl.BlockSpec((tm, LANES), lambda i, j: (i % ntab, 0)),
            pl.BlockSpec((tm, LANES), lambda i, j: (i % ntab, 0)),
        ],
        out_specs=pl.BlockSpec((tm, tn), lambda i, j: (i, j)),
        out_shape=jax.ShapeDtypeStruct((n, cols), BF16),
        compiler_params=_cparams(("arbitrary", "arbitrary")),
        name="q_proj",
    )(zp, w_uq_perm, ctab, stab)


def _mm_kernel(x_ref, w_ref, o_ref):
    o_ref[...] = _dot(x_ref[...], w_ref[...]).astype(o_ref.dtype)


def _mm(x, x_block, k, w, *, tm, tn, out_dtype, name):
    n = x.shape[0]
    cols = w.shape[1]
    return pl.pallas_call(
        _mm_kernel,
        grid=(n // tm, cols // tn),
        in_specs=[
            pl.BlockSpec((tm, k), lambda i, j: (i, x_block)),
            pl.BlockSpec((k, tn), lambda i, j: (0, j)),
        ],
        out_specs=pl.BlockSpec((tm, tn), lambda i, j: (i, j)),
        out_shape=jax.ShapeDtypeStruct((n, cols), out_dtype),
        compiler_params=_cparams(("arbitrary", "arbitrary")),
        name=name,
    )(x, w)


def _head_mm_kernel(x_ref, w_ref, o_ref, *, contract_w_cols):
    dot = _dot_nt if contract_w_cols else _dot
    x = x_ref[...]
    if x.ndim == 3:
        x = x.astype(F32).reshape(x.shape[0] * x.shape[1], x.shape[2])
    res = dot(x, w_ref[...])
    o_ref[...] = res.reshape(o_ref.shape).astype(o_ref.dtype)


def _head_mm(x, w, heads, c, w_block0, *, contract_w_cols, seq_out, out_dtype, name):
    r = w.shape[0]
    m = r if contract_w_cols else c
    if x.ndim == 3:
        nseq, rows, k = x.shape
        t = rows // heads
        n = nseq * t
        x_spec = pl.BlockSpec((nseq, t, k), lambda i: (0, i, 0))
    else:
        n = x.shape[0]
        x_spec = pl.BlockSpec((n, x.shape[1] // heads), lambda i: (0, i))
    if seq_out is None:
        out_spec = pl.BlockSpec((n, m), lambda i: (0, i))
        out_shape = jax.ShapeDtypeStruct((n, heads * m), out_dtype)
    else:
        nseq, t = seq_out
        out_spec = pl.BlockSpec((nseq, t, m), lambda i: (0, i, 0))
        out_shape = jax.ShapeDtypeStruct((nseq, heads * t, m), out_dtype)
    return pl.pallas_call(
        functools.partial(_head_mm_kernel, contract_w_cols=contract_w_cols),
        grid=(heads,),
        in_specs=[x_spec, pl.BlockSpec((r, c), lambda i: (0, w_block0 + i))],
        out_specs=out_spec,
        out_shape=out_shape,
        compiler_params=_cparams(("arbitrary",)),
        name=name,
    )(x, w)


HEADS_PER_STEP = 8
ROW_SPLIT = 2


def _softmax_update(s, m_prev, l_prev):
    m_next = jnp.maximum(m_prev, jnp.max(s, axis=-1, keepdims=True))
    alpha = jnp.exp(m_prev - m_next)
    p = jnp.exp(s - m_next)
    l_next = alpha * l_prev + jnp.sum(p, axis=-1, keepdims=True)
    return m_next, l_next, alpha, p.astype(BF16)


def _attn_prompt_kernel(qi_ref, ki_ref, qn_ref, qp_ref, kn_ref, v_ref, kpe_ref, o_ref,
                        m_ref, l_ref, acc_ref, *, tq, tk):
    step = pl.program_id(2)
    qi = qi_ref[step]
    ki = ki_ref[step]
    heads = range(HEADS_PER_STEP)

    @pl.when(ki == 0)
    def _():
        m_ref[...] = jnp.full(m_ref.shape, NEG_INF, F32)
        l_ref[...] = jnp.zeros(l_ref.shape, F32)
        acc_ref[...] = jnp.zeros(acc_ref.shape, F32)

    def block(diagonal):
        kpe = kpe_ref[...].astype(BF16)
        rq = tq // ROW_SPLIT
        units = [(h, r * rq) for h in heads for r in range(ROW_SPLIT)]
        s = [(_dot_nt(qn_ref[r0:r0 + rq, h * QK_NOPE:(h + 1) * QK_NOPE],
                      kn_ref[:, h * QK_NOPE:(h + 1) * QK_NOPE])
              + _dot_nt(qp_ref[r0:r0 + rq, h * QK_ROPE:(h + 1) * QK_ROPE], kpe)) * SM_SCALE
             for h, r0 in units]
        if diagonal:
            s = [jnp.where(lax.broadcasted_iota(jnp.int32, (rq, tk), 1)
                           <= lax.broadcasted_iota(jnp.int32, (rq, tk), 0) + r0, su, NEG_INF)
                 for su, (h, r0) in zip(s, units)]
        upd = [_softmax_update(su, m_ref[h, r0:r0 + rq], l_ref[h, r0:r0 + rq])
               for su, (h, r0) in zip(s, units)]
        for (m_next, l_next, alpha, p), (h, r0) in zip(upd, units):
            m_ref[h, r0:r0 + rq] = m_next
            l_ref[h, r0:r0 + rq] = l_next
            acc_ref[h, r0:r0 + rq] = (alpha * acc_ref[h, r0:r0 + rq]
                                      + _dot(p, v_ref[:, h * V_HEAD:(h + 1) * V_HEAD]))

    @pl.when(ki < qi)
    def _():
        block(False)

    @pl.when(ki == qi)
    def _():
        block(True)
        for h in heads:
            o_ref[:, h * V_HEAD:(h + 1) * V_HEAD] = acc_ref[h] / l_ref[h]


def _attn_prompt(q, kv, kpe, *, batch, seq, tq):
    tk = tq
    nq = seq // tq
    qi_tab = jnp.asarray([qi for qi in range(nq) for _ in range(qi + 1)], jnp.int32)
    ki_tab = jnp.asarray([ki for qi in range(nq) for ki in range(qi + 1)], jnp.int32)
    n_pairs = MLA_HEADS // HEADS_PER_STEP
    nope_w = HEADS_PER_STEP * QK_NOPE
    pe_w = HEADS_PER_STEP * QK_ROPE
    v_w = HEADS_PER_STEP * V_HEAD
    pe_off = MLA_HEADS * QK_NOPE // pe_w
    v_off = MLA_HEADS * QK_NOPE // v_w
    grid_spec = pltpu.PrefetchScalarGridSpec(
        num_scalar_prefetch=2,
        grid=(batch, n_pairs, int(qi_tab.shape[0])),
        in_specs=[
            pl.BlockSpec((tq, nope_w), lambda b, hp, s, qt, kt: (b * nq + qt[s], hp)),
            pl.BlockSpec((tq, pe_w), lambda b, hp, s, qt, kt: (b * nq + qt[s], pe_off + hp)),
            pl.BlockSpec((tk, nope_w), lambda b, hp, s, qt, kt: (b * nq + kt[s], hp)),
            pl.BlockSpec((tk, v_w), lambda b, hp, s, qt, kt: (b * nq + kt[s], v_off + hp)),
            pl.BlockSpec((tk, QK_ROPE), lambda b, hp, s, qt, kt: (b * nq + kt[s], 0)),
        ],
        out_specs=pl.BlockSpec((tq, v_w), lambda b, hp, s, qt, kt: (b * nq + qt[s], hp)),
        scratch_shapes=[
            pltpu.VMEM((HEADS_PER_STEP, tq, 1), F32),
            pltpu.VMEM((HEADS_PER_STEP, tq, 1), F32),
            pltpu.VMEM((HEADS_PER_STEP, tq, V_HEAD), F32),
        ],
    )
    return pl.pallas_call(
        functools.partial(_attn_prompt_kernel, tq=tq, tk=tk),
        grid_spec=grid_spec,
        out_shape=jax.ShapeDtypeStruct((batch * seq, MLA_HEADS * V_HEAD), F32),
        compiler_params=_cparams(("arbitrary", "arbitrary", "arbitrary")),
        name="attn_prompt",
    )(qi_tab, ki_tab, q, q, kv, kv, kpe)


PAGE_SLOTS = 3


def _attn_paged_kernel(pt_ref, ql_ref, qp_ref, cn_ref, kn_ref, ckv_hbm, kpet_hbm, o_ref,
                       kbuf, pbuf, sem, m_ref, l_ref, acc_ref, *, pages, page, t_new):
    b = pl.program_id(0)
    g = pl.program_id(1)
    nb = pl.num_programs(0)
    ng = pl.num_programs(1)
    lin = b * ng + g

    groups = [(b, g, lax.rem(lin, PAGE_SLOTS))]
    for _ in range(PAGE_SLOTS - 1):
        pb, pg, ps = groups[-1]
        wrap = pg + 1 == ng
        groups.append((jnp.where(wrap, jnp.where(pb + 1 == nb, 0, pb + 1), pb),
                       jnp.where(wrap, 0, pg + 1),
                       jnp.where(ps + 1 == PAGE_SLOTS, 0, ps + 1)))

    def copies(ahead):
        bb, gg, sl = groups[ahead]
        out = []
        for i in range(pages):
            pid = pt_ref[bb, gg * pages + i]
            out.append(pltpu.make_async_copy(ckv_hbm.at[pid], kbuf.at[sl, i], sem.at[0, sl]))
            out.append(pltpu.make_async_copy(kpet_hbm.at[pid], pbuf.at[sl, i], sem.at[1, sl]))
        return out

    @pl.when(lin == 0)
    def _():
        for n in range(PAGE_SLOTS - 1):
            for k, c in enumerate(copies(n)):
                c.start(priority=(k // 2) % 2)

    @pl.when(g == 0)
    def _():
        m_ref[...] = jnp.full(m_ref.shape, NEG_INF, F32)
        l_ref[...] = jnp.zeros(l_ref.shape, F32)
        acc_ref[...] = jnp.zeros(acc_ref.shape, F32)

    for c in copies(0):
        c.wait()

    slot = groups[0][2]
    ahead = copies(PAGE_SLOTS - 1)
    ql = ql_ref[0]
    qp = qp_ref[0]
    keys, s = [], []
    for i in range(pages):
        ahead[2 * i].start(priority=i % 2)
        ahead[2 * i + 1].start(priority=i % 2)
        keys.append(kbuf[slot, i].astype(BF16))
        s.append(_dot_nt(ql, keys[i]) + _dot(qp, pbuf[slot, i]))
    half = pages // 2
    m_run, l_run, acc = m_ref[...], l_ref[...], acc_ref[...]
    for lo, hi in ((0, half), (half, pages)):
        s_part = jnp.concatenate(s[lo:hi], axis=-1) * SM_SCALE
        m_run, l_run, alpha, p = _softmax_update(s_part, m_run, l_run)
        pv = _dot(p[:, 0:page], keys[lo])
        for i in range(lo + 1, hi):
            pv = pv + _dot(p[:, (i - lo) * page:(i - lo + 1) * page], keys[i])
        acc = alpha * acc + pv
    m_ref[...] = m_run
    l_ref[...] = l_run
    acc_ref[...] = acc

    @pl.when(lin == nb * ng - 1)
    def _():
        for n in range(1, PAGE_SLOTS):
            for c in copies(n):
                c.wait()

    @pl.when(g == ng - 1)
    def _():
        cn = cn_ref[0].astype(BF16)
        rows = ql.shape[0]
        sn = (_dot_nt(ql, cn) + _dot_nt(qp, kn_ref[0])) * SM_SCALE
        t_row = lax.broadcasted_iota(jnp.int32, (rows, t_new), 0) % t_new
        t_col = lax.broadcasted_iota(jnp.int32, (rows, t_new), 1)
        sn = jnp.where(t_col <= t_row, sn, NEG_INF)
        m_prev = m_ref[...]
        m_next = jnp.maximum(m_prev, jnp.max(sn, axis=-1, keepdims=True))
        alpha = jnp.exp(m_prev - m_next)
        p = jnp.exp(sn - m_next)
        l = alpha * l_ref[...] + jnp.sum(p, axis=-1, keepdims=True)
        acc = alpha * acc_ref[...] + _dot(p, cn)
        o_ref[0] = (acc / l).astype(o_ref.dtype)


def _attn_paged(page_table, q_lat, q_pe, ckv_new, kpe_new, cache_ckv, cache_kpe_t, *, pages):
    nseq, rows, kv_lora = q_lat.shape
    t_new = ckv_new.shape[1]
    page = cache_ckv.shape[1]
    n_pages = page_table.shape[1]
    assert n_pages % pages == 0 and pages % 2 == 0
    in_specs = [
        pl.BlockSpec((1, rows, kv_lora), lambda b, g, pt: (b, 0, 0)),
        pl.BlockSpec((1, rows, QK_ROPE), lambda b, g, pt: (b, 0, 0)),
        pl.BlockSpec((1, t_new, kv_lora), lambda b, g, pt: (b, 0, 0)),
        pl.BlockSpec((1, t_new, QK_ROPE), lambda b, g, pt: (b, 0, 0)),
        pl.BlockSpec(memory_space=pl.ANY),
        pl.BlockSpec(memory_space=pl.ANY),
    ]
    grid_spec = pltpu.PrefetchScalarGridSpec(
        num_scalar_prefetch=1,
        grid=(nseq, n_pages // pages),
        in_specs=in_specs,
        out_specs=pl.BlockSpec((1, rows, kv_lora), lambda b, g, pt: (b, 0, 0)),
        scratch_shapes=[
            pltpu.VMEM((PAGE_SLOTS, pages, page, kv_lora), F32),
            pltpu.VMEM((PAGE_SLOTS, pages, QK_ROPE, page), F32),
            pltpu.SemaphoreType.DMA((2, PAGE_SLOTS)),
            pltpu.VMEM((rows, 1), F32),
            pltpu.VMEM((rows, 1), F32),
            pltpu.VMEM((rows, kv_lora), F32),
        ],
    )
    return pl.pallas_call(
        functools.partial(_attn_paged_kernel, pages=pages, page=page, t_new=t_new),
        grid_spec=grid_spec,
        out_shape=jax.ShapeDtypeStruct((nseq, rows, kv_lora), BF16),
        compiler_params=_cparams(("arbitrary", "arbitrary")),
        name="attn_paged",
    )(page_table, q_lat, q_pe, ckv_new, kpe_new, cache_ckv, cache_kpe_t)


def _rwkv_kernel(r_ref, k_ref, v_ref, misc_ref, shift_ref, s0_ref,
                 mu_ref, w0_ref, a0_ref, kkw_ref, kaw_ref, rk_ref, gng_ref, gnb_ref,
                 w2_ref, a2_ref, g2_ref,
                 o_ref, sout_ref,
                 prev_ref, st_ref, wl_ref, rt_ref, kt_ref, vv_ref, at_ref, bt_ref, bon_ref, gate_ref,
                 y_ref, *, chunk, heads, head_group):
    c = pl.program_id(1)
    nc = pl.num_programs(1)
    d = r_ref.shape[1]
    misc_w = misc_ref.shape[1]
    lora_w = w2_ref.shape[0]

    @pl.when(c == 0)
    def _():
        prev_ref[...] = shift_ref[0]
        st_ref[...] = s0_ref[0]

    sub = SUBLANES
    row0 = lax.broadcasted_iota(jnp.int32, (sub, 1), 0) == 0

    def mix(z, lo, width):
        rolled = pltpu.roll(z, 1, 0)
        first = jnp.where(row0, prev_ref[:, lo:lo + width], rolled[:sub])
        prev = first if chunk == sub else jnp.concatenate([first, rolled[sub:]], axis=0)
        return z + (prev - z) * mu_ref[:, lo:lo + width]

    r_raw = r_ref[...]
    k_raw = k_ref[...]
    v_raw = v_ref[...]
    m_raw = misc_ref[...]
    r = mix(r_raw, 0, d)
    k = mix(k_raw, d, d)
    v = mix(v_raw, 2 * d, d)
    m = mix(m_raw, 3 * d, misc_w)
    last = chunk - 1
    prev_ref[:, 0:d] = r_raw[last:last + 1]
    prev_ref[:, d:2 * d] = k_raw[last:last + 1]
    prev_ref[:, 2 * d:3 * d] = v_raw[last:last + 1]
    prev_ref[:, 3 * d:3 * d + misc_w] = m_raw[last:last + 1]

    g_lo = m[:, 0:GATE_LORA]
    lo = m[:, GATE_LORA:GATE_LORA + lora_w]
    w_pre = w0_ref[...] + _dot(jnp.tanh(lo), w2_ref[...])
    a = _sigmoid(a0_ref[...] + _dot(lo, a2_ref[...]))
    gate = _dot(_sigmoid(g_lo), g2_ref[...])
    lw = -DECAY_SCALE * _sigmoid(w_pre)
    ti = lax.broadcasted_iota(jnp.int32, (chunk, chunk), 0)
    tj = lax.broadcasted_iota(jnp.int32, (chunk, chunk), 1)
    incl = ti >= tj
    strict = ti > tj
    tri = incl.astype(BF16)
    hi = lw.astype(BF16)
    mid = (lw - hi.astype(F32)).astype(BF16)
    low = (lw - hi.astype(F32) - mid.astype(F32)).astype(BF16)
    cs = (jnp.dot(tri, hi, preferred_element_type=F32) + jnp.dot(tri, mid, preferred_element_type=F32)
          + jnp.dot(tri, low, preferred_element_type=F32))
    gi_ = lax.broadcasted_iota(jnp.int32, (SEG_BLK, SEG_BLK), 0) // RW_HEAD
    gj_ = lax.broadcasted_iota(jnp.int32, (SEG_BLK, SEG_BLK), 1) // RW_HEAD
    seg = (gi_ == gj_).astype(BF16)

    def head_sum(x):
        x_hi = x.astype(BF16).astype(F32)
        x_lo = x - x_hi
        cols = range(0, d, SEG_BLK)
        stacked = jnp.concatenate([x_hi[:, j:j + SEG_BLK] for j in cols]
                                  + [x_lo[:, j:j + SEG_BLK] for j in cols], axis=0)
        sums = jnp.dot(stacked.astype(BF16), seg, preferred_element_type=F32)
        nblk = len(cols)
        return jnp.concatenate(
            [sums[j * chunk:(j + 1) * chunk] + sums[(nblk + j) * chunk:(nblk + j + 1) * chunk]
             for j in range(nblk)], axis=1)

    w_inv = jnp.exp(-cs)
    kp = k * (1.0 + (a - 1.0) * kaw_ref[...])
    kkr = k * kkw_ref[...]
    kk = kkr * lax.rsqrt(jnp.maximum(head_sum(kkr * kkr), KK_NORM_FLOOR * KK_NORM_FLOOR))
    w_c = jnp.exp(cs)
    rt_ref[...] = r * w_c
    kt_ref[...] = kp * w_inv
    vv_ref[...] = v
    at_ref[...] = -kk * jnp.exp(cs - lw)
    bt_ref[...] = kk * a * w_inv
    wl_ref[...] = w_c[last:last + 1]
    bon_ref[...] = head_sum(r * kp * rk_ref[...])
    gate_ref[...] = gate

    n_double = max(1, (chunk - 1).bit_length())
    gw = head_group * RW_HEAD

    def group_body(gi, carry):
        off = pl.multiple_of(gi * gw, gw)
        sl = pl.ds(off, gw)
        rt_g = rt_ref[:, sl]
        kt_g = kt_ref[:, sl]
        v_g = vv_ref[:, sl]
        at_g = at_ref[:, sl]
        bt_g = bt_ref[:, sl]
        wl_g = wl_ref[:, sl]
        s0s = [st_ref[gi * head_group + i] for i in range(head_group)]
        hrange = range(head_group)
        hsl = [slice(i * RW_HEAD, (i + 1) * RW_HEAD) for i in hrange]
        vh = [v_g[:, hs] for hs in hsl]
        lhs = [jnp.concatenate([at_g[:, hs], rt_g[:, hs]], axis=0) for hs in hsl]
        rhs = [jnp.concatenate([bt_g[:, hs], kt_g[:, hs]], axis=0) for hs in hsl]
        gram = [_dot_nt(lhs[i], rhs[i]) for i in hrange]
        h0 = [_dot_nt(lhs[i], s0s[i]) for i in hrange]
        pw = [jnp.where(strict, gram[i][:chunk, :chunk], 0.0) for i in hrange]
        u = [h0[i][:chunk] + _dot(jnp.where(strict, gram[i][:chunk, chunk:], 0.0), vh[i]) for i in hrange]
        for step in range(n_double):
            u = [u[i] + _dot(pw[i], u[i]) for i in hrange]
            if step + 1 < n_double:
                pw = [_dot(pw[i], pw[i]) for i in hrange]
        uv = [jnp.concatenate([u[i], vh[i]], axis=0) for i in hrange]
        ys = []
        for i in hrange:
            a_r = jnp.concatenate([jnp.where(incl, gram[i][chunk:, :chunk], 0.0),
                                   jnp.where(incl, gram[i][chunk:, chunk:], 0.0)], axis=1)
            ys.append(h0[i][chunk:] + _dot(a_r, uv[i]))
        s_new = [(s0s[i] + _dot_tn(uv[i], rhs[i])) * wl_g[:, hsl[i]] for i in hrange]
        y_ref[:, sl] = jnp.concatenate(ys, axis=-1)
        for i in range(head_group):
            st_ref[gi * head_group + i] = s_new[i]
        return carry

    lax.fori_loop(0, heads // head_group, group_body, 0)
    y = y_ref[...]
    yc = y - head_sum(y) * (1.0 / RW_HEAD)
    var = head_sum(yc * yc) * (1.0 / RW_HEAD)
    yn = yc * lax.rsqrt(var + GN_EPS) * gng_ref[...] + gnb_ref[...]
    o_ref[...] = (yn + bon_ref[...] * vv_ref[...]) * gate_ref[...]

    @pl.when(c == nc - 1)
    def _():
        sout_ref[0] = st_ref[...]


def _rwkv(zp, misc_block, shift_perm, s0, vecs, w2p, a2p, g2, *, batch, seq, chunk, head_group):
    d = g2.shape[1]
    heads = d // RW_HEAD
    nc = seq // chunk
    misc_w = IN_BLK
    row = lambda b, c: b * nc + c
    full = lambda shape: pl.BlockSpec(shape, lambda b, c: (0,) * len(shape))
    mu, w0, a0, kkw, kaw, rk, gng, gnb = vecs
    big = pltpu.VMEM((chunk, d), F32)
    return pl.pallas_call(
        functools.partial(_rwkv_kernel, chunk=chunk, heads=heads, head_group=head_group),
        grid=(batch, nc),
        in_specs=[
            pl.BlockSpec((chunk, d), lambda b, c: (row(b, c), 0)),
            pl.BlockSpec((chunk, d), lambda b, c: (row(b, c), 1)),
            pl.BlockSpec((chunk, d), lambda b, c: (row(b, c), 2)),
            pl.BlockSpec((chunk, misc_w), lambda b, c: (row(b, c), misc_block)),
            pl.BlockSpec((1, 1, 3 * d + misc_w), lambda b, c: (b, 0, 0)),
            pl.BlockSpec((1, heads, RW_HEAD, RW_HEAD), lambda b, c: (b, 0, 0, 0)),
            full((1, 3 * d + misc_w)),
            full((1, d)), full((1, d)), full((1, d)), full((1, d)), full((1, d)), full((1, d)), full((1, d)),
            full(w2p.shape), full(a2p.shape), full(g2.shape),
        ],
        out_specs=[
            pl.BlockSpec((chunk, d), lambda b, c: (row(b, c), 0)),
            pl.BlockSpec((1, heads, RW_HEAD, RW_HEAD), lambda b, c: (b, 0, 0, 0)),
        ],
        out_shape=[
            jax.ShapeDtypeStruct((batch * seq, d), F32),
            jax.ShapeDtypeStruct((batch, heads, RW_HEAD, RW_HEAD), F32),
        ],
        scratch_shapes=[
            pltpu.VMEM((1, 3 * d + misc_w), F32),
            pltpu.VMEM((heads, RW_HEAD, RW_HEAD), F32),
            pltpu.VMEM((1, d), F32),
            big, big, big, big, big, big, big, big,
        ],
        compiler_params=_cparams(("arbitrary", "arbitrary")),
        name="rwkv",
    )(zp, zp, zp, zp, shift_perm, s0, mu, w0, a0, kkw, kaw, rk, gng, gnb, w2p, a2p, g2)


def _layer_norm(x, g, b):
    mu = jnp.mean(x, axis=-1, keepdims=True)
    xc = x - mu
    var = jnp.mean(xc * xc, axis=-1, keepdims=True)
    return xc * lax.rsqrt(var + LN_EPS) * g + b


def _out_proj_kernel(gm_ref, gr_ref, om_ref, or_ref, x_ref, w_ref, g_ref, b_ref, o_ref, *, alpha):
    merged = gm_ref[...] * om_ref[...] + gr_ref[...] * or_ref[...]
    y = alpha * x_ref[...] + _dot(merged, w_ref[...])
    o_ref[...] = _layer_norm(y, g_ref[...], b_ref[...])


def _out_proj(zp, o_mla, o_rw, x, w_o, ln_g, ln_b, *, alpha, tm):
    n, d = x.shape
    rows = lambda blk: pl.BlockSpec((tm, d), lambda i: (i, blk))
    const = lambda shape: pl.BlockSpec(shape, lambda i: (0, 0))
    return pl.pallas_call(
        functools.partial(_out_proj_kernel, alpha=alpha),
        grid=(n // tm,),
        in_specs=[rows(3), rows(4), rows(0), rows(0), rows(0), const((d, d)), const((1, d)), const((1, d))],
        out_specs=rows(0),
        out_shape=jax.ShapeDtypeStruct((n, d), F32),
        compiler_params=_cparams(("arbitrary",)),
        name="out_proj",
    )(zp, zp, o_mla, o_rw, x, w_o, ln_g, ln_b)


def _ffn_kernel(h_ref, wg_ref, wu_ref, wd_ref, g_ref, b_ref, wpg_ref, pe_ref, wpe_ref, o_ref,
                hb_ref, acc_ref, *, alpha):
    j = pl.program_id(1)

    @pl.when(j == 0)
    def _():
        hb_ref[...] = h_ref[...].astype(BF16)
        acc_ref[...] = jnp.zeros(acc_ref.shape, F32)

    hb = hb_ref[...]
    gate = jnp.dot(hb, wg_ref[...], preferred_element_type=F32)
    up = jnp.dot(hb, wu_ref[...], preferred_element_type=F32)
    acc_ref[...] += _dot(gate * _sigmoid(gate) * up, wd_ref[...])

    @pl.when(j == pl.num_programs(1) - 1)
    def _():
        h2 = _layer_norm(alpha * h_ref[...] + acc_ref[...], g_ref[...], b_ref[...])
        o_ref[...] = h2 + _sigmoid(_dot(h2, wpg_ref[...])) * _dot(pe_ref[...], wpe_ref[...])


def _ffn(h, wg, wu, wd, ln_g, ln_b, w_pg, pe, w_pe, *, alpha, tm, tf):
    n, d = h.shape
    d_ff = wg.shape[1]
    p = pe.shape[1]
    return pl.pallas_call(
        functools.partial(_ffn_kernel, alpha=alpha),
        grid=(n // tm, d_ff // tf),
        in_specs=[
            pl.BlockSpec((tm, d), lambda i, j: (i, 0)),
            pl.BlockSpec((d, tf), lambda i, j: (0, j)),
            pl.BlockSpec((d, tf), lambda i, j: (0, j)),
            pl.BlockSpec((tf, d), lambda i, j: (j, 0)),
            pl.BlockSpec((1, d), lambda i, j: (0, 0)),
            pl.BlockSpec((1, d), lambda i, j: (0, 0)),
            pl.BlockSpec((d, d), lambda i, j: (0, 0)),
            pl.BlockSpec((tm, p), lambda i, j: (i, 0)),
            pl.BlockSpec((p, d), lambda i, j: (0, 0)),
        ],
        out_specs=pl.BlockSpec((tm, d), lambda i, j: (i, 0)),
        out_shape=jax.ShapeDtypeStruct((n, d), F32),
        scratch_shapes=[pltpu.VMEM((tm, d), BF16), pltpu.VMEM((tm, d), F32)],
        compiler_params=_cparams(("arbitrary", "arbitrary")),
        name="ffn",
    )(h, wg, wu, wd, ln_g, ln_b, w_pg, pe, w_pe)


def _ple_kernel(h_ref, wpg_ref, pe_ref, wpe_ref, o_ref, hb_ref, *, tn):
    j = pl.program_id(1)

    @pl.when(j == 0)
    def _():
        hb_ref[...] = h_ref[...].astype(BF16)

    gate = _sigmoid(jnp.dot(hb_ref[...], wpg_ref[...], preferred_element_type=F32))
    emb = _dot(pe_ref[...], wpe_ref[...])
    o_ref[...] = h_ref[:, pl.ds(pl.multiple_of(j * tn, tn), tn)] + gate * emb


def _ple(h, w_pg, pe, w_pe, *, tm, tn):
    n, d = h.shape
    p = pe.shape[1]
    return pl.pallas_call(
        functools.partial(_ple_kernel, tn=tn),
        grid=(n // tm, d // tn),
        in_specs=[
            pl.BlockSpec((tm, d), lambda i, j: (i, 0)),
            pl.BlockSpec((d, tn), lambda i, j: (0, j)),
            pl.BlockSpec((tm, p), lambda i, j: (i, 0)),
            pl.BlockSpec((p, tn), lambda i, j: (0, j)),
        ],
        out_specs=pl.BlockSpec((tm, tn), lambda i, j: (i, j)),
        out_shape=jax.ShapeDtypeStruct((n, d), F32),
        scratch_shapes=[pltpu.VMEM((tm, d), BF16)],
        compiler_params=_cparams(("arbitrary", "arbitrary")),
        name="ple",
    )(h, w_pg, pe, w_pe)


def _rope_tables(pos):
    half = QK_ROPE // 2
    inv = ROPE_THETA ** (-jnp.arange(half, dtype=F32) / half)
    ang = pos[:, None] * inv[None, :]
    cos = jnp.cos(ang)
    sin = jnp.sin(ang)
    return jnp.concatenate([cos, cos], axis=-1), jnp.concatenate([-sin, sin], axis=-1)


def _tile_rows(t, tm):
    reps = max(1, tm // t.shape[0])
    return jnp.tile(t, (reps, 1)) if reps > 1 else t


def kernel(x_prompt, x_sample, p_prompt, p_sample, cache_ckv, cache_kpe, state_wkv, state_shift, page_table, w_in, mu_shift, g_q, g_kv, w_uq, w_uk, w_uv, rw_w0, rw_w2, rw_a0, rw_a2, rw_g2, rw_kk, rw_ka, rw_rk, gn_g, gn_b, w_o, ln1_g, ln1_b, w_ffn_gate, w_ffn_up, w_ffn_down, ln2_g, ln2_b, w_ple, w_ple_gate):
    depth = w_in.shape[0]
    assert depth == 1, "single-layer trunk"
    bp, tp, d = x_prompt.shape
    bs, ts, _ = x_sample.shape
    q_lora = g_q.shape[1]
    kv_lora = g_kv.shape[1]
    page = cache_ckv.shape[2]
    past_len = page_table.shape[1] * page
    rw_cols = mu_shift.shape[1]
    mla_cols = q_lora + kv_lora + QK_ROPE
    off_gate = mla_cols + rw_cols
    alpha = (2.0 * depth) ** 0.25
    assert d // RW_HEAD * RW_HEAD == d and q_lora == IN_BLK and kv_lora == IN_BLK
    assert GATE_LORA + DECAY_LORA + AAA_LORA + QK_ROPE == IN_BLK and 2 * QK_ROPE == LANES
    assert QK_NOPE == V_HEAD

    o_r, o_wlo, o_k, o_v = 0, d, d + DECAY_LORA, 2 * d + DECAY_LORA
    o_alo = 3 * d + DECAY_LORA
    o_glo = o_alo + AAA_LORA

    def rw_perm(t):
        return jnp.concatenate([
            t[..., o_r:o_r + d], t[..., o_k:o_k + d], t[..., o_v:o_v + d],
            t[..., o_glo:o_glo + GATE_LORA], t[..., o_wlo:o_wlo + DECAY_LORA],
            t[..., o_alo:o_alo + AAA_LORA]], axis=-1)

    wi = w_in[0].astype(BF16)
    w_rw = rw_perm(wi[:, mla_cols:off_gate])
    w_perm = jnp.concatenate([
        w_rw[:, :3 * d], wi[:, off_gate:], wi[:, :q_lora + kv_lora],
        w_rw[:, 3 * d:], wi[:, q_lora + kv_lora:mla_cols]], axis=-1)
    cq_block = 5 * d // IN_BLK
    ckv_block = cq_block + 1
    misc_block = cq_block + 2
    ones = jnp.ones((1, IN_BLK), F32)
    gains = jnp.concatenate(
        [jnp.ones((1, 5 * d), F32), g_q[0][None], g_kv[0][None], ones], axis=-1)
    zeros_kpe = jnp.zeros((1, QK_ROPE), F32)
    mu_perm = jnp.concatenate([rw_perm(mu_shift[0])[None], zeros_kpe], axis=-1)

    wq = w_uq[0]
    w_uq_perm = jnp.concatenate([
        wq[:, :, :QK_NOPE].reshape(q_lora, MLA_HEADS * QK_NOPE),
        wq[:, :, QK_NOPE:].reshape(q_lora, MLA_HEADS * QK_ROPE)], axis=-1).astype(BF16)
    w_kv = jnp.concatenate([
        w_uk[0].reshape(kv_lora, MLA_HEADS * QK_NOPE),
        w_uv[0].reshape(kv_lora, MLA_HEADS * V_HEAD)], axis=-1).astype(BF16)

    lora_w = IN_BLK - GATE_LORA
    w2p = jnp.zeros((lora_w, d), F32).at[:DECAY_LORA].set(rw_w2[0]).astype(BF16)
    a2p = jnp.zeros((lora_w, d), F32).at[DECAY_LORA:DECAY_LORA + AAA_LORA].set(rw_a2[0]).astype(BF16)
    g2 = rw_g2[0].astype(BF16)
    vecs = (mu_perm, rw_w0, rw_a0, rw_kk, rw_ka, rw_rk[0].reshape(1, d), gn_g, gn_b)

    w_o_b = w_o[0].astype(BF16)
    wg_b = w_ffn_gate[0].astype(BF16)
    wu_b = w_ffn_up[0].astype(BF16)
    wd_b = w_ffn_down[0].astype(BF16)
    w_pg_b = w_ple_gate[0].astype(BF16)
    w_pe_b = w_ple[0].astype(BF16)

    def trunk(x3, pe3, pos, shift_prev, wkv_prev, chunk, head_group, attend):
        b, t, _ = x3.shape
        n = b * t
        x = x3.reshape(n, d)
        tl = _tiles(n, t, page_table.shape[1])
        tm = tl["in_proj_rows"]
        cos64, sin64 = _rope_tables(pos)
        pad = IN_BLK - QK_ROPE
        c_misc = _tile_rows(jnp.concatenate([jnp.ones_like(cos64), cos64], axis=1), tm)
        s_misc = _tile_rows(jnp.concatenate([jnp.zeros_like(sin64), sin64], axis=1), tm)
        zp = _in_proj(x, w_perm, gains, c_misc, s_misc, d_model=d, tm=tm)
        ckv = zp[:, ckv_block * IN_BLK:(ckv_block + 1) * IN_BLK]
        kpe = zp[:, misc_block * IN_BLK + pad:]
        tmq = tl["q_proj_rows"]
        c_q = _tile_rows(jnp.concatenate([cos64, cos64], axis=1), tmq)
        s_q = _tile_rows(jnp.concatenate([sin64, sin64], axis=1), tmq)
        q = _q_proj(zp, cq_block, w_uq_perm, c_q, s_q, tm=tmq)
        o_mla = attend(zp, q, ckv, kpe, tl)
        shift_perm = jnp.concatenate(
            [rw_perm(shift_prev), jnp.zeros((b, QK_ROPE), F32)], axis=-1)[:, None, :]
        o_rw, wkv_new = _rwkv(zp, misc_block, shift_perm, wkv_prev, vecs, w2p, a2p, g2,
                              batch=b, seq=t, chunk=chunk, head_group=head_group)
        h1 = _out_proj(zp, o_mla, o_rw, x, w_o_b, ln1_g, ln1_b, alpha=alpha, tm=tl["out_proj_rows"])
        out = _ffn(h1, wg_b, wu_b, wd_b, ln2_g, ln2_b, w_pg_b, pe3.reshape(n, -1), w_pe_b,
                   alpha=alpha, tm=tl["ffn_rows"], tf=tl["ffn_cols"])
        last = zp.reshape(b, t, -1)[:, -1]
        rw_last = jnp.concatenate([last[:, :3 * d], last[:, misc_block * IN_BLK:misc_block * IN_BLK + pad]], -1)
        shift_new = jnp.concatenate([
            rw_last[:, 0:d], rw_last[:, 3 * d + GATE_LORA:3 * d + GATE_LORA + DECAY_LORA],
            rw_last[:, d:3 * d], rw_last[:, 3 * d + GATE_LORA + DECAY_LORA:],
            rw_last[:, 3 * d:3 * d + GATE_LORA]], axis=-1)
        return (out.reshape(b, t, d), ckv.reshape(b, t, kv_lora), kpe.reshape(b, t, QK_ROPE),
                shift_new, wkv_new)

    def attend_prompt(zp, q, ckv, kpe, tl):
        kv = _mm(zp, ckv_block, kv_lora, w_kv, tm=tl["kv_proj_rows"], tn=tl["kv_proj_cols"],
                 out_dtype=BF16, name="kv_proj")
        return _attn_prompt(q, kv, kpe, batch=bp, seq=tp, tq=tl["attn_q_rows"])

    n_phys = cache_ckv.shape[1]
    cache_c = cache_ckv.reshape(n_phys, page, kv_lora)
    cache_r = jnp.swapaxes(cache_kpe.reshape(n_phys, page, QK_ROPE), 1, 2)

    def attend_sample(zp, q, ckv, kpe, tl):
        nope_cols = MLA_HEADS * QK_NOPE
        q_lat = _head_mm(q[:, :nope_cols], w_kv, MLA_HEADS, QK_NOPE, 0, contract_w_cols=True,
                         seq_out=(bs, ts), out_dtype=BF16, name="q_lat")
        q_pe = jnp.swapaxes(q[:, nope_cols:].reshape(bs, ts, MLA_HEADS, QK_ROPE), 1, 2)
        q_pe = q_pe.reshape(bs, MLA_HEADS * ts, QK_ROPE)
        o_lat = _attn_paged(page_table, q_lat, q_pe, ckv.reshape(bs, ts, kv_lora),
                            kpe.reshape(bs, ts, QK_ROPE), cache_c, cache_r,
                            pages=tl["pages_per_step"])
        return _head_mm(o_lat, w_kv, MLA_HEADS, V_HEAD, MLA_HEADS, contract_w_cols=False,
                        seq_out=None, out_dtype=F32, name="o_lat")

    pos_p = jnp.arange(tp, dtype=F32)
    pos_s = past_len + jnp.arange(ts, dtype=F32)
    shift0 = jnp.zeros((bp, rw_cols), F32)
    wkv0 = jnp.zeros((bp, d // RW_HEAD, RW_HEAD, RW_HEAD), F32)
    hp, c1, k1, s1, w1 = trunk(x_prompt, p_prompt[0], pos_p, shift0, wkv0, min(PROMPT_CHUNK, tp),
                               RW_GROUP_PROMPT, attend_prompt)
    hs, c2, k2, s2, w2 = trunk(x_sample, p_sample[0], pos_s, state_shift[0], state_wkv[0], ts,
                               RW_GROUP_SAMPLE, attend_sample)
    return (hp, hs, c1[None], k1[None], w1[None], s1[None], c2[None], k2[None], w2[None], s2[None])
```
